```python
import math
import jax, jax.numpy as jnp
from jax import lax
import numpy as np

D_MODEL = 2048
BATCH = 4
SEQ = 2048
DEPTH = 4
DEC_BATCH = 8
DEC_SEQ = 8
PAST_LEN = 16384
PAGE_SIZE = 128

HEAD_DIM = 64
ATTN_WIDTH = D_MODEL // 2
N_HEADS = ATTN_WIDTH // (2 * HEAD_DIM)
QK_DIM = 2 * HEAD_DIM
V_DIM = 2 * HEAD_DIM
CONV_CH = D_MODEL // 2
CONV_K = 31
CONV_STATE = CONV_K - 1
FFN_DIM = 256 * ((8 * D_MODEL // 3 + 255) // 256)
FFN_CONV_K = 3
FFN_STATE = FFN_CONV_K - 1
N_MEM = 256
X_HEADS = 4
X_HEAD_DIM = D_MODEL // X_HEADS
Q_BLOCK = 128
ALPHA = (2.0 * DEPTH) ** 0.25
BETA = (8.0 * DEPTH) ** -0.25
LN_EPS = 1e-5
IN_SPLITS = (ATTN_WIDTH, ATTN_WIDTH, N_HEADS * V_DIM, 2 * CONV_CH, D_MODEL, D_MODEL)
IN_WIDTH = sum(IN_SPLITS)

kernel_name = 'hybrid_diffattn_conformer_convffn_step'


def layer_norm(x, g, b):
    xf = x.astype(jnp.float32)
    mu = jnp.mean(xf, axis=-1, keepdims=True)
    var = jnp.mean(jnp.square(xf - mu), axis=-1, keepdims=True)
    return ((xf - mu) * lax.rsqrt(var + LN_EPS) * g + b).astype(x.dtype)


def rms_norm(x, g):
    xf = x.astype(jnp.float32)
    return (xf * lax.rsqrt(jnp.mean(jnp.square(xf), axis=-1, keepdims=True) + LN_EPS) * g).astype(x.dtype)


def project_in(x, w):
    n, t, _ = x.shape
    h = x @ w
    parts = []
    off = 0
    for size in IN_SPLITS:
        parts.append(h[..., off:off + size])
        off += size
    q, k, v, glu, ga, gb = parts
    q = q.reshape(n, t, N_HEADS, 2, HEAD_DIM)
    k = k.reshape(n, t, N_HEADS, 2, HEAD_DIM)
    v = v.reshape(n, t, N_HEADS, V_DIM)
    u = glu[..., :CONV_CH] * jax.nn.sigmoid(glu[..., CONV_CH:])
    return q, k, v, u, jax.nn.sigmoid(ga), jax.nn.sigmoid(gb)


def diff_lambda(lam, lam_init):
    lf = lam.astype(jnp.float32)
    return jnp.exp(jnp.sum(lf[0] * lf[1])) - jnp.exp(jnp.sum(lf[2] * lf[3])) + lam_init


def diff_probs(s, lam_full):
    p = jax.nn.softmax(s, axis=-1)
    return p[..., 0, :, :] - lam_full * p[..., 1, :, :]


def diff_attn_prompt(q, k, v, lam_full):
    n, t = q.shape[:2]
    nb = t // Q_BLOCK
    qb = q.reshape(n, nb, Q_BLOCK, N_HEADS, 2, HEAD_DIM).transpose(1, 0, 2, 3, 4, 5)
    kpos = jnp.arange(t)
    scale = HEAD_DIM ** -0.5

    def block(args):
        qi, start = args
        s = jnp.einsum('bqhcd,bkhcd->bhcqk', qi, k, preferred_element_type=jnp.float32) * scale
        qpos = start + jnp.arange(Q_BLOCK)
        s = jnp.where(kpos[None, :] <= qpos[:, None], s, -jnp.inf)
        p = diff_probs(s, lam_full).astype(v.dtype)
        return jnp.einsum('bhqk,bkhe->bqhe', p, v)

    out = lax.map(block, (qb, jnp.arange(nb) * Q_BLOCK))
    return out.transpose(1, 0, 2, 3, 4).reshape(n, t, N_HEADS, V_DIM)


def diff_attn_sample(q, k_new, v_new, k_past, v_past, lam_full):
    t = q.shape[1]
    p_len = k_past.shape[1]
    scale = HEAD_DIM ** -0.5
    s_past = jnp.einsum('bqhcd,bkhcd->bhcqk', q, k_past, preferred_element_type=jnp.float32) * scale
    s_new = jnp.einsum('bqhcd,bkhcd->bhcqk', q, k_new, preferred_element_type=jnp.float32) * scale
    causal = jnp.arange(t)[None, :] <= jnp.arange(t)[:, None]
    s_new = jnp.where(causal, s_new, -jnp.inf)
    p = diff_probs(jnp.concatenate([s_past, s_new], axis=-1), lam_full).astype(v_new.dtype)
    return (jnp.einsum('bhqk,bkhe->bqhe', p[..., :p_len], v_past)
            + jnp.einsum('bhqk,bkhe->bqhe', p[..., p_len:], v_new))


def depthwise_valid(x_ext, w, b):
    c = x_ext.shape[-1]
    out = lax.conv_general_dilated(x_ext, w[:, None, :], window_strides=(1,), padding='VALID',
                                   dimension_numbers=('NWC', 'WIO', 'NWC'), feature_group_count=c)
    return out + b


def mix_merge(att, u_ext, ga, gb, lam_init, sub_g, w_a, cw, cb, cln_g, cln_b, w_b, w_o):
    n, t = att.shape[:2]
    a_branch = (rms_norm(att, sub_g) * (1.0 - lam_init)).reshape(n, t, N_HEADS * V_DIM) @ w_a
    c = depthwise_valid(u_ext, cw, cb)
    c_branch = jax.nn.silu(layer_norm(c, cln_g, cln_b)) @ w_b
    return (ga * a_branch + gb * c_branch) @ w_o


def mem_kv(mem, w_xk, w_xv):
    n, m, _ = mem.shape
    return ((mem @ w_xk).reshape(n, m, X_HEADS, X_HEAD_DIM),
            (mem @ w_xv).reshape(n, m, X_HEADS, X_HEAD_DIM))


def cross_attn(x, mk, mv, w_xq, w_xo):
    n, t, _ = x.shape
    q = (x @ w_xq).reshape(n, t, X_HEADS, X_HEAD_DIM)
    s = jnp.einsum('bqhd,bmhd->bhqm', q, mk, preferred_element_type=jnp.float32) * (X_HEAD_DIM ** -0.5)
    p = jax.nn.softmax(s, axis=-1).astype(x.dtype)
    return jnp.einsum('bhqm,bmhd->bqhd', p, mv).reshape(n, t, D_MODEL) @ w_xo


def conv_ffn(x, a_prev, w_up, cw, cb, w_down):
    h = x @ w_up
    a_ext = jnp.concatenate([a_prev, h[..., :FFN_DIM]], axis=1)
    a_c = depthwise_valid(a_ext, cw, cb)
    y = (jax.nn.gelu(a_c, approximate=False) * h[..., FFN_DIM:]) @ w_down
    return y, a_ext[:, -FFN_STATE:]


def setup_inputs(seed: int = 0) -> dict:
    key = jax.random.key(seed)
    ks = iter(jax.random.split(key, 48))

    def nrm(shape, scale):
        return jax.random.normal(next(ks), shape, jnp.float32) * scale

    n_pages = PAST_LEN // PAGE_SIZE
    n_used = DEC_BATCH * n_pages
    n_pool = n_used + n_used // 4
    page_table = jax.random.permutation(next(ks), n_pool)[:n_used].reshape(DEC_BATCH, n_pages).astype(jnp.int32)
    d = D_MODEL
    return {
        'x_prompt': nrm((BATCH, SEQ, d), 1.0),
        'x_sample': nrm((DEC_BATCH, DEC_SEQ, d), 1.0),
        'cache_k': nrm((DEPTH, n_pool, PAGE_SIZE, N_HEADS, QK_DIM), 1.0),
        'cache_v': nrm((DEPTH, n_pool, PAGE_SIZE, N_HEADS, V_DIM), 1.0),
        'cache_mem_k': nrm((DEPTH, DEC_BATCH, N_MEM, X_HEADS, X_HEAD_DIM), 1.0),
        'cache_mem_v': nrm((DEPTH, DEC_BATCH, N_MEM, X_HEADS, X_HEAD_DIM), BETA),
        'state_conv': nrm((DEPTH, DEC_BATCH, CONV_STATE, CONV_CH), 0.5),
        'state_ffn': nrm((DEPTH, DEC_BATCH, FFN_STATE, FFN_DIM), 1.0),
        'page_table': page_table,
        'mem_prompt': nrm((BATCH, N_MEM, d), 1.0),
        'w_in': nrm((DEPTH, d, IN_WIDTH), d ** -0.5),
        'lam': nrm((DEPTH, 4, HEAD_DIM), 0.1),
        'subln_g': 1.0 + nrm((DEPTH, V_DIM), 0.02),
        'w_a': nrm((DEPTH, N_HEADS * V_DIM, d), (N_HEADS * V_DIM) ** -0.5),
        'conv_w': nrm((DEPTH, CONV_K, CONV_CH), CONV_K ** -0.5),
        'conv_b': nrm((DEPTH, CONV_CH), 0.02),
        'conv_ln_g': 1.0 + nrm((DEPTH, CONV_CH), 0.02),
        'conv_ln_b': nrm((DEPTH, CONV_CH), 0.02),
        'w_b': nrm((DEPTH, CONV_CH, d), CONV_CH ** -0.5),
        'w_o': nrm((DEPTH, d, d), BETA * d ** -0.5),
        'ln1_g': 1.0 + nrm((DEPTH, d), 0.02),
        'ln1_b': nrm((DEPTH, d), 0.02),
        'w_xq': nrm((DEPTH, d, d), d ** -0.5),
        'w_xk': nrm((DEPTH, d, d), d ** -0.5),
        'w_xv': nrm((DEPTH, d, d), BETA * d ** -0.5),
        'w_xo': nrm((DEPTH, d, d), BETA * d ** -0.5),
        'ln2_g': 1.0 + nrm((DEPTH, d), 0.02),
        'ln2_b': nrm((DEPTH, d), 0.02),
        'w_up': nrm((DEPTH, d, 2 * FFN_DIM), d ** -0.5),
        'ffn_conv_w': nrm((DEPTH, FFN_CONV_K, FFN_DIM), FFN_CONV_K ** -0.5),
        'ffn_conv_b': nrm((DEPTH, FFN_DIM), 0.02),
        'w_down': nrm((DEPTH, FFN_DIM, d), BETA * FFN_DIM ** -0.5),
        'ln3_g': 1.0 + nrm((DEPTH, d), 0.02),
        'ln3_b': nrm((DEPTH, d), 0.02),
    }


def reference(x_prompt, x_sample, cache_k, cache_v, cache_mem_k, cache_mem_v, state_conv, state_ffn,
              page_table, mem_prompt,
              w_in, lam, subln_g, w_a, conv_w, conv_b, conv_ln_g, conv_ln_b, w_b, w_o, ln1_g, ln1_b,
              w_xq, w_xk, w_xv, w_xo, ln2_g, ln2_b,
              w_up, ffn_conv_w, ffn_conv_b, w_down, ln3_g, ln3_b):
    xp, xs = x_prompt, x_sample
    n_p, t_p, _ = xp.shape
    n_s, t_s, _ = xs.shape
    kp_l, vp_l, mkp_l, mvp_l, cp_l, fp_l = [], [], [], [], [], []
    ks_l, vs_l, cs_l, fs_l = [], [], [], []
    for l in range(DEPTH):
        lam_init = 0.8 - 0.6 * math.exp(-0.3 * l)
        lam_full = diff_lambda(lam[l], lam_init)
        mix_w = (lam_init, subln_g[l], w_a[l], conv_w[l], conv_b[l], conv_ln_g[l], conv_ln_b[l], w_b[l], w_o[l])

        q, k, v, u, ga, gb = project_in(xp, w_in[l])
        att = diff_attn_prompt(q, k, v, lam_full)
        u_ext = jnp.concatenate([jnp.zeros((n_p, CONV_STATE, CONV_CH), u.dtype), u], axis=1)
        xp = layer_norm(ALPHA * xp + mix_merge(att, u_ext, ga, gb, *mix_w), ln1_g[l], ln1_b[l])
        kp_l.append(k.reshape(n_p, t_p, N_HEADS, QK_DIM))
        vp_l.append(v)
        cp_l.append(u_ext[:, -CONV_STATE:])

        q, k, v, u, ga, gb = project_in(xs, w_in[l])
        k_past = cache_k[l, page_table].reshape(n_s, -1, N_HEADS, 2, HEAD_DIM)
        v_past = cache_v[l, page_table].reshape(n_s, -1, N_HEADS, V_DIM)
        att = diff_attn_sample(q, k, v, k_past, v_past, lam_full)
        u_ext = jnp.concatenate([state_conv[l], u], axis=1)
        xs = layer_norm(ALPHA * xs + mix_merge(att, u_ext, ga, gb, *mix_w), ln1_g[l], ln1_b[l])
        ks_l.append(k.reshape(n_s, t_s, N_HEADS, QK_DIM))
        vs_l.append(v)
        cs_l.append(u_ext[:, -CONV_STATE:])

        mk, mv = mem_kv(mem_prompt, w_xk[l], w_xv[l])
        xp = layer_norm(ALPHA * xp + cross_attn(xp, mk, mv, w_xq[l], w_xo[l]), ln2_g[l], ln2_b[l])
        xs = layer_norm(ALPHA * xs + cross_attn(xs, cache_mem_k[l], cache_mem_v[l], w_xq[l], w_xo[l]),
                        ln2_g[l], ln2_b[l])
        mkp_l.append(mk)
        mvp_l.append(mv)

        f, a_last = conv_ffn(xp, jnp.zeros((n_p, FFN_STATE, FFN_DIM), xp.dtype),
                             w_up[l], ffn_conv_w[l], ffn_conv_b[l], w_down[l])
        xp = layer_norm(ALPHA * xp + f, ln3_g[l], ln3_b[l])
        fp_l.append(a_last)
        f, a_last = conv_ffn(xs, state_ffn[l], w_up[l], ffn_conv_w[l], ffn_conv_b[l], w_down[l])
        xs = layer_norm(ALPHA * xs + f, ln3_g[l], ln3_b[l])
        fs_l.append(a_last)

    return (xp, xs,
            jnp.stack(kp_l), jnp.stack(vp_l), jnp.stack(mkp_l), jnp.stack(mvp_l),
            jnp.stack(cp_l), jnp.stack(fp_l),
            jnp.stack(ks_l), jnp.stack(vs_l), jnp.stack(cs_l), jnp.stack(fs_l))
```

```python
import functools
import math

import jax
import jax.numpy as jnp
from jax import lax
from jax.experimental import pallas as pl
from jax.experimental.pallas import tpu as pltpu

F32 = jnp.float32
BF16 = jnp.bfloat16

DEPTH = 4
HEAD_DIM = 64
HEAD_W = 2 * HEAD_DIM
CONV_K = 31
CONV_STATE = CONV_K - 1
CONV_PAD = 32
FFN_CONV_K = 3
X_HEADS = 4
ALPHA = (2.0 * DEPTH) ** 0.25
LN_EPS = 1e-5
QK_SCALE = HEAD_DIM ** -0.5

V7X_VMEM_LIMIT_BYTES = 56 * 1024 * 1024
SUBLANES = 8
LANES = 128

_NT = (((1,), (1,)), ((), ()))


def _params(*semantics):
    return pltpu.CompilerParams(dimension_semantics=semantics,
                                vmem_limit_bytes=V7X_VMEM_LIMIT_BYTES)


def _layer_norm(y, g, b):
    mu = jnp.mean(y, axis=-1, keepdims=True)
    var = jnp.mean(jnp.square(y - mu), axis=-1, keepdims=True)
    return (y - mu) * lax.rsqrt(var + LN_EPS) * g + b


def _dot(a, b):
    return jnp.dot(a, b, preferred_element_type=F32)


def _lambda_full(lam_ref, lam_init):
    lf = lam_ref[...]
    s1 = jnp.sum(lf[0:1] * lf[1:2], axis=-1, keepdims=True)
    s2 = jnp.sum(lf[2:3] * lf[3:4], axis=-1, keepdims=True)
    return jnp.exp(s1) - jnp.exp(s2) + lam_init


def _sub_norm(o, g, lam_init):
    r = o * lax.rsqrt(jnp.mean(jnp.square(o), axis=-1, keepdims=True) + LN_EPS) * g
    return r * (1.0 - lam_init)


def _inproj_kernel(x_ref, wq, wk, wv, wua, wub, wga, wgb,
                   q_o, k_o, v_o, kb_o, vb_o, u_o, ga_o, gb_o, xb):
    @pl.when(pl.program_id(1) == 0)
    def _():
        xb[...] = x_ref[...].astype(BF16)

    x = xb[...]
    q_o[...] = (_dot(x, wq[...]) * QK_SCALE).astype(BF16)
    k = _dot(x, wk[...])
    k_o[...] = k
    kb_o[...] = k.astype(BF16)
    v = _dot(x, wv[...])
    v_o[...] = v
    vb_o[...] = v.astype(BF16)
    u_o[...] = _dot(x, wua[...]) * jax.nn.sigmoid(_dot(x, wub[...]))
    ga_o[...] = jax.nn.sigmoid(_dot(x, wga[...]))
    gb_o[...] = jax.nn.sigmoid(_dot(x, wgb[...]))


def _in_proj(x, w_in, tm, tn=256):
    m, d = x.shape
    aw = d // 2
    nj = aw // tn
    tg = 2 * tn
    offs = (0, aw, 2 * aw, 3 * aw, 4 * aw)
    w_specs = [pl.BlockSpec((d, tn), functools.partial(lambda i, j, o: (0, o + j), o=o // tn))
               for o in offs]
    w_specs += [pl.BlockSpec((d, tg), functools.partial(lambda i, j, o: (0, o + j), o=o // tg))
                for o in (5 * aw, 5 * aw + d)]
    narrow = pl.BlockSpec((tm, tn), lambda i, j: (i, j))
    wide = pl.BlockSpec((tm, tg), lambda i, j: (i, j))
    return pl.pallas_call(
        _inproj_kernel,
        grid=(m // tm, nj),
        in_specs=[pl.BlockSpec((tm, d), lambda i, j: (i, 0))] + w_specs,
        out_specs=[narrow] * 6 + [wide] * 2,
        out_shape=[jax.ShapeDtypeStruct((m, aw), BF16),
                   jax.ShapeDtypeStruct((m, aw), F32),
                   jax.ShapeDtypeStruct((m, aw), F32),
                   jax.ShapeDtypeStruct((m, aw), BF16),
                   jax.ShapeDtypeStruct((m, aw), BF16),
                   jax.ShapeDtypeStruct((m, aw), F32),
                   jax.ShapeDtypeStruct((m, d), F32),
                   jax.ShapeDtypeStruct((m, d), F32)],
        scratch_shapes=[pltpu.VMEM((tm, d), BF16)],
        compiler_params=_params("arbitrary", "arbitrary"),
        name="in_proj",
    )(x, *([w_in] * 7))


def _dattn_kernel(lam_ref, g_ref, q_ref, k_ref, v_ref, o_ref, *, tq, lam_init):
    qi = pl.program_id(2)
    q = q_ref[...]
    q1, q2 = q[:, :HEAD_DIM], q[:, HEAD_DIM:]
    qpos = qi * tq + lax.broadcasted_iota(jnp.int32, (tq, tq), 0)
    kcol = lax.broadcasted_iota(jnp.int32, (tq, tq), 1)

    def one_map(qh, kh, vc, mask, m, l, acc):
        s = lax.dot_general(qh, kh, _NT, preferred_element_type=F32)
        s = jnp.where(mask, s, -jnp.inf)
        m_new = jnp.maximum(m, jnp.max(s, axis=-1, keepdims=True))
        p = jnp.exp(s - m_new)
        a = jnp.exp(m - m_new)
        l = a * l + jnp.sum(p, axis=-1, keepdims=True)
        acc = a * acc + _dot(p.astype(BF16), vc)
        return m_new, l, acc

    def body(c, carry):
        m1, l1, a1, m2, l2, a2 = carry
        off = pl.multiple_of(c * tq, tq)
        kc = k_ref[pl.ds(off, tq), :]
        vc = v_ref[pl.ds(off, tq), :]
        mask = (off + kcol) <= qpos
        m1, l1, a1 = one_map(q1, kc[:, :HEAD_DIM], vc, mask, m1, l1, a1)
        m2, l2, a2 = one_map(q2, kc[:, HEAD_DIM:], vc, mask, m2, l2, a2)
        return m1, l1, a1, m2, l2, a2

    neg = jnp.full((tq, 1), -jnp.inf, F32)
    zero = jnp.zeros((tq, 1), F32)
    zacc = jnp.zeros((tq, HEAD_W), F32)
    _, l1, a1, _, l2, a2 = lax.fori_loop(0, qi + 1, body, (neg, zero, zacc, neg, zero, zacc))
    o = a1 / l1 - _lambda_full(lam_ref, lam_init) * (a2 / l2)
    o_ref[...] = _sub_norm(o, g_ref[...], lam_init).astype(BF16)


def _diff_attn_prompt(q, k, v, lam_l, sub_g, n, t, lam_init, tq=256):
    m, aw = q.shape
    nq = t // tq
    nh = aw // HEAD_W
    return pl.pallas_call(
        functools.partial(_dattn_kernel, tq=tq, lam_init=lam_init),
        grid=(n, nh, nq),
        in_specs=[pl.BlockSpec(lam_l.shape, lambda b, h, i: (0, 0)),
                  pl.BlockSpec((1, HEAD_W), lambda b, h, i: (0, 0)),
                  pl.BlockSpec((tq, HEAD_W), lambda b, h, i: (b * nq + i, h)),
                  pl.BlockSpec((t, HEAD_W), lambda b, h, i: (b, h)),
                  pl.BlockSpec((t, HEAD_W), lambda b, h, i: (b, h))],
        out_specs=pl.BlockSpec((tq, HEAD_W), lambda b, h, i: (b * nq + i, h)),
        out_shape=jax.ShapeDtypeStruct((m, aw), BF16),
        compiler_params=_params("arbitrary", "arbitrary", "arbitrary"),
        name="diff_attn_prompt",
    )(lam_l, sub_g, q, k, v)


def _sattn_kernel(pt_ref, lam_ref, g_ref, q_ref, kn_ref, vn_ref, *rest,
                  pages_per_step, lam_init):
    del pt_ref
    k_refs = rest[:pages_per_step]
    v_refs = rest[pages_per_step:2 * pages_per_step]
    o_ref, qbd, knp, vnp, m_s, l_s, acc_s = rest[2 * pages_per_step:]
    p_id = pl.program_id(1)
    t_new, aw = q_ref.shape[1], q_ref.shape[2]
    nh = aw // HEAD_W
    rows = 2 * t_new
    page = knp.shape[0]

    def update(kp, vp, mask):
        s = jnp.concatenate(
            [lax.dot_general(qbd[h * rows:(h + 1) * rows, :], kp[:, h * HEAD_W:(h + 1) * HEAD_W],
                             _NT, preferred_element_type=F32) for h in range(nh)], axis=0)
        if mask is not None:
            s = jnp.where(mask, s, -jnp.inf)
        m_old = m_s[...]
        m_new = jnp.maximum(m_old, jnp.max(s, axis=-1, keepdims=True))
        p = jnp.exp(s - m_new)
        a = jnp.exp(m_old - m_new)
        l_s[...] = a * l_s[...] + jnp.sum(p, axis=-1, keepdims=True)
        p = p.astype(BF16)
        pv = jnp.concatenate(
            [_dot(p[h * rows:(h + 1) * rows, :], vp[:, h * HEAD_W:(h + 1) * HEAD_W])
             for h in range(nh)], axis=0)
        acc_s[...] = a * acc_s[...] + pv
        m_s[...] = m_new

    @pl.when(p_id == 0)
    def _():
        lane = lax.broadcasted_iota(jnp.int32, (t_new, HEAD_W), 1)
        for h in range(nh):
            qh = q_ref[0, :, h * HEAD_W:(h + 1) * HEAD_W]
            qbd[h * rows:h * rows + t_new, :] = jnp.where(lane < HEAD_DIM, qh, jnp.zeros_like(qh))
            qbd[h * rows + t_new:(h + 1) * rows, :] = jnp.where(lane >= HEAD_DIM, qh, jnp.zeros_like(qh))
        knp[...] = jnp.zeros_like(knp)
        vnp[...] = jnp.zeros_like(vnp)
        knp[0:t_new, :] = kn_ref[0]
        vnp[0:t_new, :] = vn_ref[0]
        m_s[...] = jnp.full_like(m_s, -jnp.inf)
        l_s[...] = jnp.zeros_like(l_s)
        acc_s[...] = jnp.zeros_like(acc_s)
        r = lax.broadcasted_iota(jnp.int32, (nh * rows, page), 0)
        j = lax.broadcasted_iota(jnp.int32, (nh * rows, page), 1)
        update(knp[...], vnp[...], j <= r % t_new)

    for kr, vr in zip(k_refs, v_refs):
        update(kr[...].astype(BF16), vr[...].astype(BF16), None)

    @pl.when(p_id == pl.num_programs(1) - 1)
    def _():
        o_all = acc_s[...] / l_s[...]
        lam_full = _lambda_full(lam_ref, lam_init)
        for h in range(nh):
            o1 = o_all[h * rows:h * rows + t_new, :]
            o2 = o_all[h * rows + t_new:(h + 1) * rows, :]
            o_ref[0, :, h * HEAD_W:(h + 1) * HEAD_W] = _sub_norm(
                o1 - lam_full * o2, g_ref[...], lam_init).astype(BF16)


def _diff_attn_sample(q, k_new, v_new, cache_k, cache_v, page_table, lam_l, sub_g, layer, lam_init,
                      pages_per_step=4):
    n, t_new, aw = q.shape
    page = cache_k.shape[2]
    n_pages = page_table.shape[1]
    nh = aw // HEAD_W
    rows_all = nh * 2 * t_new

    def page_spec(i):
        return pl.BlockSpec((None, None, page, aw),
                            lambda b, p, pt: (layer, pt[b, p * pages_per_step + i], 0, 0))

    tok_spec = pl.BlockSpec((1, t_new, aw), lambda b, p, pt: (b, 0, 0))
    grid_spec = pltpu.PrefetchScalarGridSpec(
        num_scalar_prefetch=1,
        grid=(n, n_pages // pages_per_step),
        in_specs=[pl.BlockSpec(lam_l.shape, lambda b, p, pt: (0, 0)),
                  pl.BlockSpec((1, HEAD_W), lambda b, p, pt: (0, 0)),
                  tok_spec, tok_spec, tok_spec]
                 + [page_spec(i) for i in range(pages_per_step)] * 2,
        out_specs=tok_spec,
        scratch_shapes=[pltpu.VMEM((rows_all, HEAD_W), BF16),
                        pltpu.VMEM((page, aw), BF16),
                        pltpu.VMEM((page, aw), BF16),
                        pltpu.VMEM((rows_all, 1), F32),
                        pltpu.VMEM((rows_all, 1), F32),
                        pltpu.VMEM((rows_all, HEAD_W), F32)],
    )
    return pl.pallas_call(
        functools.partial(_sattn_kernel, pages_per_step=pages_per_step, lam_init=lam_init),
        grid_spec=grid_spec,
        out_shape=jax.ShapeDtypeStruct((n, t_new, aw), BF16),
        compiler_params=_params("arbitrary", "arbitrary"),
        name="diff_attn_sample",
    )(page_table, lam_l, sub_g, q, k_new, v_new,
      *([cache_k] * pages_per_step), *([cache_v] * pages_per_step))


def _convb_kernel(u_ref, w_ref, b_ref, g_ref, be_ref, o_ref, conv_s, *, tt, rt):
    base = pl.multiple_of(pl.program_id(1) * tt, SUBLANES)
    ch = o_ref.shape[1]
    for cb in range(ch // LANES):
        cs = slice(cb * LANES, (cb + 1) * LANES)

        def rows(r, _):
            r0 = pl.multiple_of(base + r * rt, SUBLANES)
            win = u_ref[0, pl.ds(r0, rt + CONV_PAD), cs]
            y = b_ref[:, cs] + win[CONV_PAD:CONV_PAD + rt] * w_ref[CONV_PAD:CONV_PAD + 1, cs]
            for s in range(SUBLANES):
                z = win[0:rt + SUBLANES] * w_ref[s:s + 1, cs]
                for a in range(1, CONV_PAD // SUBLANES):
                    o = a * SUBLANES + s
                    z = z + win[a * SUBLANES:a * SUBLANES + rt + SUBLANES] * w_ref[o:o + 1, cs]
                y = y + z[s:s + rt]
            conv_s[pl.ds(pl.multiple_of(r * rt, SUBLANES), rt), cs] = y
            return 0

        lax.fori_loop(0, tt // rt, rows, 0)
    y = _layer_norm(conv_s[...], g_ref[...], be_ref[...])
    o_ref[...] = (y * jax.nn.sigmoid(y)).astype(BF16)


def _conv_branch(u_ext, conv_w, conv_b, ln_g, ln_b, tt):
    n, t_ext, ch = u_ext.shape
    t = t_ext - CONV_PAD
    nt = t // tt
    vec = pl.BlockSpec((1, ch), lambda b, i: (0, 0))
    return pl.pallas_call(
        functools.partial(_convb_kernel, tt=tt, rt=min(tt, 32)),
        grid=(n, nt),
        in_specs=[pl.BlockSpec((1, t_ext, ch), lambda b, i: (b, 0, 0)),
                  pl.BlockSpec(conv_w.shape, lambda b, i: (0, 0)), vec, vec, vec],
        out_specs=pl.BlockSpec((tt, ch), lambda b, i: (b * nt + i, 0)),
        out_shape=jax.ShapeDtypeStruct((n * t, ch), BF16),
        scratch_shapes=[pltpu.VMEM((tt, ch), F32)],
        compiler_params=_params("arbitrary", "arbitrary"),
        name="conv_branch",
    )(u_ext, conv_w, conv_b, ln_g, ln_b)


def _merge_kernel(a_ref, c_ref, wa_ref, wb_ref, ga_ref, gb_ref, o_ref):
    o_ref[...] = (ga_ref[...] * _dot(a_ref[...], wa_ref[...])
                  + gb_ref[...] * _dot(c_ref[...], wb_ref[...])).astype(BF16)


def _merge(att, cbr, w_a, w_b, ga, gb, tm, tn=512):
    m, kw = att.shape
    d = w_a.shape[1]
    act = pl.BlockSpec((tm, kw), lambda i, j: (i, 0))
    wsp = pl.BlockSpec((kw, tn), lambda i, j: (0, j))
    tile = pl.BlockSpec((tm, tn), lambda i, j: (i, j))
    return pl.pallas_call(
        _merge_kernel,
        grid=(m // tm, d // tn),
        in_specs=[act, act, wsp, wsp, tile, tile],
        out_specs=tile,
        out_shape=jax.ShapeDtypeStruct((m, d), BF16),
        compiler_params=_params("arbitrary", "arbitrary"),
        name="merge",
    )(att, cbr, w_a, w_b, ga, gb)


def _proj_ln_kernel(x_ref, y_ref, w_ref, g_ref, b_ref, o_ref, acc):
    k = pl.program_id(1)

    @pl.when(k == 0)
    def _():
        acc[...] = jnp.zeros_like(acc)

    acc[...] += _dot(y_ref[...], w_ref[...])

    @pl.when(k == pl.num_programs(1) - 1)
    def _():
        o_ref[...] = _layer_norm(ALPHA * x_ref[...] + acc[...], g_ref[...], b_ref[...])


def _proj_ln(x, y, w, g, b, tm, tk=1024):
    m, d = x.shape
    kw = y.shape[1]
    vec = pl.BlockSpec((1, d), lambda i, k: (0, 0))
    return pl.pallas_call(
        _proj_ln_kernel,
        grid=(m // tm, kw // tk),
        in_specs=[pl.BlockSpec((tm, d), lambda i, k: (i, 0)),
                  pl.BlockSpec((tm, tk), lambda i, k: (i, k)),
                  pl.BlockSpec((tk, d), lambda i, k: (k, 0)), vec, vec],
        out_specs=pl.BlockSpec((tm, d), lambda i, k: (i, 0)),
        out_shape=jax.ShapeDtypeStruct((m, d), F32),
        scratch_shapes=[pltpu.VMEM((tm, d), F32)],
        compiler_params=_params("arbitrary", "arbitrary"),
        name="proj_ln",
    )(x, y, w, g, b)


def _proj_kernel(x_ref, w_ref, *refs):
    o_refs, xb = refs[:-1], refs[-1]

    @pl.when(pl.program_id(1) == 0)
    def _():
        xb[...] = x_ref[...].astype(BF16)

    y = _dot(xb[...], w_ref[...])
    for o in o_refs:
        o[...] = y.astype(o.dtype)


def _proj(x, w, out_dtypes, tm, tn=512):
    m, d = x.shape
    nw = w.shape[1]
    tile = pl.BlockSpec((tm, tn), lambda i, j: (i, j))
    return pl.pallas_call(
        _proj_kernel,
        grid=(m // tm, nw // tn),
        in_specs=[pl.BlockSpec((tm, d), lambda i, j: (i, 0)),
                  pl.BlockSpec((d, tn), lambda i, j: (0, j))],
        out_specs=[tile] * len(out_dtypes),
        out_shape=[jax.ShapeDtypeStruct((m, nw), dt) for dt in out_dtypes],
        scratch_shapes=[pltpu.VMEM((tm, d), BF16)],
        compiler_params=_params("arbitrary", "arbitrary"),
        name="proj",
    )(x, w)


def _xattn_kernel(q_ref, mk_ref, mv_ref, o_ref, kb, vb):
    @pl.when(pl.program_id(1) == 0)
    def _():
        kb[...] = mk_ref[...].astype(BF16)
        vb[...] = mv_ref[...].astype(BF16)

    hd = q_ref.shape[1] // X_HEADS
    for h in range(X_HEADS):
        cs = slice(h * hd, (h + 1) * hd)
        s = lax.dot_general(q_ref[:, cs], kb[:, cs], _NT, preferred_element_type=F32) * hd ** -0.5
        e = jnp.exp(s - jnp.max(s, axis=-1, keepdims=True))
        p = e / jnp.sum(e, axis=-1, keepdims=True)
        o_ref[:, cs] = _dot(p.astype(BF16), vb[:, cs]).astype(BF16)


def _cross_attn(q, mk, mv, n, t, n_mem, mem_block0, tq):
    m, d = q.shape
    nq = t // tq
    mem = pl.BlockSpec((n_mem, d), lambda b, i: (mem_block0 + b, 0))
    return pl.pallas_call(
        _xattn_kernel,
        grid=(n, nq),
        in_specs=[pl.BlockSpec((tq, d), lambda b, i: (b * nq + i, 0)), mem, mem],
        out_specs=pl.BlockSpec((tq, d), lambda b, i: (b * nq + i, 0)),
        out_shape=jax.ShapeDtypeStruct((m, d), BF16),
        scratch_shapes=[pltpu.VMEM((n_mem, d), BF16), pltpu.VMEM((n_mem, d), BF16)],
        compiler_params=_params("arbitrary", "arbitrary"),
        name="cross_attn",
    )(q, mk, mv)


def _ffn_kernel(*refs, seg, tiles_per_seq, carried, tail_rows):
    if carried:
        (x_ref, wa_ref, wg_ref, wd_ref, cw_ref, cb_ref, g_ref, b_ref,
         o_ref, tail_ref, xb, acc, carry) = refs
    else:
        (x_ref, wa_ref, wg_ref, wd_ref, cw_ref, cb_ref, g_ref, b_ref, s0_ref, s1_ref,
         o_ref, tail_ref, xb, acc) = refs
    i, j = pl.program_id(0), pl.program_id(1)
    tm, tf = x_ref.shape[0], wa_ref.shape[1]

    @pl.when(j == 0)
    def _():
        xb[...] = x_ref[...].astype(BF16)
        acc[...] = jnp.zeros_like(acc)

    x = xb[...]
    a = _dot(x, wa_ref[...])
    gate = _dot(x, wg_ref[...])
    if carried:
        @pl.when(i == 0)
        def _():
            carry[j] = jnp.zeros((SUBLANES, tf), F32)

        prev = jnp.where(i % tiles_per_seq != 0, carry[j], 0.0)
        s0 = prev[SUBLANES - 2:SUBLANES - 1]
        s1 = prev[SUBLANES - 1:SUBLANES]
        carry[j] = a[tm - SUBLANES:]
    else:
        s0, s1 = s0_ref[...], s1_ref[...]
    t = lax.broadcasted_iota(jnp.int32, (tm, tf), 0) % seg
    p1 = jnp.where(t == 0, s1, pltpu.roll(a, 1, axis=0))
    p2 = jnp.where(t == 0, s0, jnp.where(t == 1, s1, pltpu.roll(a, 2, axis=0)))
    cw = cw_ref[...]
    a_c = cw[0:1] * p2 + cw[1:2] * p1 + cw[2:3] * a + cb_ref[...]
    y = 0.5 * a_c * (1.0 + lax.erf(a_c * math.sqrt(0.5))) * gate
    acc[...] += _dot(y.astype(BF16), wd_ref[...])
    tail_ref[...] = a[tm - tail_rows:]

    @pl.when(j == pl.num_programs(1) - 1)
    def _():
        o_ref[...] = _layer_norm(ALPHA * x_ref[...] + acc[...], g_ref[...], b_ref[...])


def _conv_ffn(x, w_up, cw, cb, w_down, g, b, seg, tm, state_rows=None, tf=512):
    m, d = x.shape
    f = w_down.shape[0]
    nj = f // tf
    carried = state_rows is None
    tail_rows = SUBLANES if carried else tm
    tiles_per_seq = max(seg // tm, 1)
    vec_d = pl.BlockSpec((1, d), lambda i, j: (0, 0))
    in_specs = [pl.BlockSpec((tm, d), lambda i, j: (i, 0)),
                pl.BlockSpec((d, tf), lambda i, j: (0, j)),
                pl.BlockSpec((d, tf), lambda i, j: (0, nj + j)),
                pl.BlockSpec((tf, d), lambda i, j: (j, 0)),
                pl.BlockSpec((FFN_CONV_K, tf), lambda i, j: (0, j)),
                pl.BlockSpec((1, tf), lambda i, j: (0, j)),
                vec_d, vec_d]
    args = [x, w_up, w_up, w_down, cw, cb, g, b]
    scratch = [pltpu.VMEM((tm, d), BF16), pltpu.VMEM((tm, d), F32)]
    if carried:
        scratch.append(pltpu.VMEM((nj, SUBLANES, tf), F32))
    else:
        in_specs += [pl.BlockSpec((tm, tf), lambda i, j: (i, j))] * 2
        args += list(state_rows)
    return pl.pallas_call(
        functools.partial(_ffn_kernel, seg=seg, tiles_per_seq=tiles_per_seq, carried=carried,
                          tail_rows=tail_rows),
        grid=(m // tm, nj),
        in_specs=in_specs,
        out_specs=[pl.BlockSpec((tm, d), lambda i, j: (i, 0)),
                   pl.BlockSpec((tail_rows, tf), lambda i, j: (i, j))],
        out_shape=[jax.ShapeDtypeStruct((m, d), F32),
                   jax.ShapeDtypeStruct((m // tm * tail_rows, f), F32)],
        scratch_shapes=scratch,
        compiler_params=_params("arbitrary", "arbitrary"),
        name="conv_ffn",
    )(*args)


def _row(v):
    return v.reshape(1, -1)


def _mix_and_xattn(x, n, t, w, l, hist, attend, mem_k, mem_v, n_mem, mem_block0, tm, tq_x):
    q, k, v, kb, vb, u, ga, gb = _in_proj(x, w["w_in"][l], tm)
    att = attend(q, kb, vb)
    ch = u.shape[1]
    u_ext = jnp.concatenate(
        [jnp.zeros((n, CONV_PAD - CONV_STATE, ch), F32), hist, u.reshape(n, t, ch)], axis=1)
    lead = CONV_PAD - CONV_STATE
    conv_w = jnp.pad(w["conv_w"][l], ((lead, CONV_PAD + SUBLANES - lead - CONV_K), (0, 0)))
    cbr = _conv_branch(u_ext, conv_w, _row(w["conv_b"][l]), _row(w["conv_ln_g"][l]),
                       _row(w["conv_ln_b"][l]), tt=min(t, 256))
    merged = _merge(att, cbr, w["w_a"][l], w["w_b"][l], ga, gb, tm)
    x = _proj_ln(x, merged, w["w_o"][l], _row(w["ln1_g"][l]), _row(w["ln1_b"][l]), tm)
    (xq,) = _proj(x, w["w_xq"][l], (BF16,), tm)
    xa = _cross_attn(xq, mem_k, mem_v, n, t, n_mem, mem_block0, tq_x)
    x = _proj_ln(x, xa, w["w_xo"][l], _row(w["ln2_g"][l]), _row(w["ln2_b"][l]), tm)
    return x, k, v, u_ext


def kernel(x_prompt, x_sample, cache_k, cache_v, cache_mem_k, cache_mem_v, state_conv, state_ffn,
           page_table, mem_prompt, w_in, lam, subln_g, w_a, conv_w, conv_b, conv_ln_g, conv_ln_b,
           w_b, w_o, ln1_g, ln1_b, w_xq, w_xk, w_xv, w_xo, ln2_g, ln2_b, w_up, ffn_conv_w,
           ffn_conv_b, w_down, ln3_g, ln3_b):
    n_p, t_p, d = x_prompt.shape
    n_s, t_s, _ = x_sample.shape
    depth = w_in.shape[0]
    aw = d // 2
    nh = aw // HEAD_W
    n_mem = mem_prompt.shape[1]
    f = w_down.shape[1]
    page = cache_k.shape[2]
    m_p, m_s = n_p * t_p, n_s * t_s
    tm_p, tm_s = min(512, m_p), m_s

    w = dict(w_in=w_in.astype(BF16), w_a=w_a.astype(BF16), w_b=w_b.astype(BF16),
             w_o=w_o.astype(BF16), w_xq=w_xq.astype(BF16), w_xk=w_xk.astype(BF16),
             w_xv=w_xv.astype(BF16), w_xo=w_xo.astype(BF16), w_up=w_up.astype(BF16),
             w_down=w_down.astype(BF16), conv_w=conv_w, conv_b=conv_b, conv_ln_g=conv_ln_g,
             conv_ln_b=conv_ln_b, ln1_g=ln1_g, ln1_b=ln1_b, ln2_g=ln2_g, ln2_b=ln2_b)
    cache_k2 = cache_k.reshape(depth, -1, page, aw)
    cache_v2 = cache_v.reshape(depth, -1, page, aw)
    mem_k_s = cache_mem_k.reshape(depth * n_s * n_mem, d)
    mem_v_s = cache_mem_v.reshape(depth * n_s * n_mem, d)
    mem2 = mem_prompt.reshape(n_p * n_mem, d)

    xp = x_prompt.reshape(m_p, d)
    xs = x_sample.reshape(m_s, d)
    outs = {name: [] for name in ("kp", "vp", "mkp", "mvp", "cp", "fp", "ks", "vs", "cs", "fs")}
    for l in range(depth):
        lam_init = 0.8 - 0.6 * math.exp(-0.3 * l)
        sub_g = _row(subln_g[l])

        mk, = _proj(mem2, w["w_xk"][l], (F32,), min(512, mem2.shape[0]))
        mv, = _proj(mem2, w["w_xv"][l], (F32,), min(512, mem2.shape[0]))
        attend_p = lambda q, kb, vb: _diff_attn_prompt(q, kb, vb, lam[l], sub_g, n_p, t_p, lam_init)
        xp, k, v, u_ext = _mix_and_xattn(
            xp, n_p, t_p, w, l, jnp.zeros((n_p, CONV_STATE, aw), F32), attend_p,
            mk, mv, n_mem, 0, tm_p, min(512, t_p))
        xp, tail = _conv_ffn(xp, w["w_up"][l], ffn_conv_w[l], _row(ffn_conv_b[l]), w["w_down"][l],
                             _row(ln3_g[l]), _row(ln3_b[l]), seg=t_p, tm=tm_p)
        outs["kp"].append(k.reshape(n_p, t_p, nh, HEAD_W))
        outs["vp"].append(v.reshape(n_p, t_p, nh, HEAD_W))
        outs["mkp"].append(mk.reshape(n_p, n_mem, X_HEADS, d // X_HEADS))
        outs["mvp"].append(mv.reshape(n_p, n_mem, X_HEADS, d // X_HEADS))
        outs["cp"].append(u_ext[:, -CONV_STATE:])
        outs["fp"].append(tail.reshape(n_p, t_p // tm_p, SUBLANES, f)[:, -1, -(FFN_CONV_K - 1):])

        def attend_s(q, kb, vb):
            att = _diff_attn_sample(q.reshape(n_s, t_s, aw), kb.reshape(n_s, t_s, aw),
                                    vb.reshape(n_s, t_s, aw), cache_k2, cache_v2, page_table,
                                    lam[l], sub_g, l, lam_init)
            return att.reshape(m_s, aw)

        xs, k, v, u_ext = _mix_and_xattn(
            xs, n_s, t_s, w, l, state_conv[l], attend_s,
            mem_k_s, mem_v_s, n_mem, l * n_s, tm_s, t_s)
        st = state_ffn[l]
        s0 = jnp.repeat(st[:, 0], t_s, axis=0)
        s1 = jnp.repeat(st[:, 1], t_s, axis=0)
        xs, a_s = _conv_ffn(xs, w["w_up"][l], ffn_conv_w[l], _row(ffn_conv_b[l]), w["w_down"][l],
                            _row(ln3_g[l]), _row(ln3_b[l]), seg=t_s, tm=tm_s, state_rows=(s0, s1))
        outs["ks"].append(k.reshape(n_s, t_s, nh, HEAD_W))
        outs["vs"].append(v.reshape(n_s, t_s, nh, HEAD_W))
        outs["cs"].append(u_ext[:, -CONV_STATE:])
        a_ext = jnp.concatenate([st, a_s.reshape(n_s, t_s, f)], axis=1)
        outs["fs"].append(a_ext[:, -(FFN_CONV_K - 1):])

    stk = {name: jnp.stack(vals) for name, vals in outs.items()}
    return (xp.reshape(n_p, t_p, d), xs.reshape(n_s, t_s, d),
            stk["kp"], stk["vp"], stk["mkp"], stk["mvp"], stk["cp"], stk["fp"],
            stk["ks"], stk["vs"], stk["cs"], stk["fs"])
```

```python
import functools
import math

import jax
import jax.numpy as jnp
from jax import lax
from jax.experimental import pallas as pl
from jax.experimental.pallas import tpu as pltpu

F32 = jnp.float32
BF16 = jnp.bfloat16

DEPTH = 4
HEAD_DIM = 64
HEAD_W = 2 * HEAD_DIM
CONV_K = 31
CONV_STATE = CONV_K - 1
CONV_PAD = 32
FFN_CONV_K = 3
X_HEADS = 4
ALPHA = (2.0 * DEPTH) ** 0.25
LN_EPS = 1e-5
QK_SCALE = HEAD_DIM ** -0.5

V7X_VMEM_LIMIT_BYTES = 56 * 1024 * 1024
SUBLANES = 8
LANES = 128

_NT = (((1,), (1,)), ((), ()))


def _params(*semantics):
    return pltpu.CompilerParams(dimension_semantics=semantics,
                                vmem_limit_bytes=V7X_VMEM_LIMIT_BYTES)


def _layer_norm(y, g, b):
    mu = jnp.mean(y, axis=-1, keepdims=True)
    var = jnp.mean(jnp.square(y - mu), axis=-1, keepdims=True)
    return (y - mu) * lax.rsqrt(var + LN_EPS) * g + b


def _dot(a, b):
    return jnp.dot(a, b, preferred_element_type=F32)


def _lambda_full(lam_ref, lam_init):
    lf = lam_ref[...]
    s1 = jnp.sum(lf[0:1] * lf[1:2], axis=-1, keepdims=True)
    s2 = jnp.sum(lf[2:3] * lf[3:4], axis=-1, keepdims=True)
    return jnp.exp(s1) - jnp.exp(s2) + lam_init


def _sub_norm(o, g, lam_init):
    r = o * lax.rsqrt(jnp.mean(jnp.square(o), axis=-1, keepdims=True) + LN_EPS) * g
    return r * (1.0 - lam_init)


def _inproj_kernel(x_ref, wq, wk, wv, wua, wub, wga, wgb,
                   q_o, k_o, v_o, kb_o, vb_o, u_o, ga_o, gb_o, xb):
    @pl.when(pl.program_id(1) == 0)
    def _():
        xb[...] = x_ref[...].astype(BF16)

    x = xb[...]
    q_o[...] = (_dot(x, wq[...]) * QK_SCALE).astype(BF16)
    k = _dot(x, wk[...])
    k_o[...] = k
    kb_o[...] = k.astype(BF16)
    v = _dot(x, wv[...])
    v_o[...] = v
    vb_o[...] = v.astype(BF16)
    u_o[...] = _dot(x, wua[...]) * jax.nn.sigmoid(_dot(x, wub[...]))
    ga_o[...] = jax.nn.sigmoid(_dot(x, wga[...])).astype(BF16)
    gb_o[...] = jax.nn.sigmoid(_dot(x, wgb[...])).astype(BF16)


def _in_proj(x, w_in, tm, tn=256):
    m, d = x.shape
    aw = d // 2
    nj = aw // tn
    tg = 2 * tn
    offs = (0, aw, 2 * aw, 3 * aw, 4 * aw)
    w_specs = [pl.BlockSpec((d, tn), functools.partial(lambda i, j, o: (0, o + j), o=o // tn))
               for o in offs]
    w_specs += [pl.BlockSpec((d, tg), functools.partial(lambda i, j, o: (0, o + j), o=o // tg))
                for o in (5 * aw, 5 * aw + d)]
    narrow = pl.BlockSpec((tm, tn), lambda i, j: (i, j))
    wide = pl.BlockSpec((tm, tg), lambda i, j: (i, j))
    return pl.pallas_call(
        _inproj_kernel,
        grid=(m // tm, nj),
        in_specs=[pl.BlockSpec((tm, d), lambda i, j: (i, 0))] + w_specs,
        out_specs=[narrow] * 6 + [wide] * 2,
        out_shape=[jax.ShapeDtypeStruct((m, aw), BF16),
                   jax.ShapeDtypeStruct((m, aw), F32),
                   jax.ShapeDtypeStruct((m, aw), F32),
                   jax.ShapeDtypeStruct((m, aw), BF16),
                   jax.ShapeDtypeStruct((m, aw), BF16),
                   jax.ShapeDtypeStruct((m, aw), F32),
                   jax.ShapeDtypeStruct((m, d), BF16),
                   jax.ShapeDtypeStruct((m, d), BF16)],
        scratch_shapes=[pltpu.VMEM((tm, d), BF16)],
        compiler_params=_params("arbitrary", "arbitrary"),
        name="in_proj",
    )(x, *([w_in] * 7))


def _dattn_kernel(lam_ref, g_ref, q_ref, k_ref, v_ref, o_ref, vt, acc, *, tq, lam_init):
    qi = pl.program_id(2)
    hps = q_ref.shape[1] // HEAD_W
    heads = [slice(h * HEAD_W, (h + 1) * HEAD_W) for h in range(hps)]

    @pl.when(qi == 0)
    def _():
        for c in range(vt.shape[1]):
            for h, hs in enumerate(heads):
                vt[h, c] = v_ref[c * tq:(c + 1) * tq, hs].astype(F32).T.astype(BF16)

    lane = lax.broadcasted_iota(jnp.int32, (tq, HEAD_W), 1)
    qm = []
    for hs in heads:
        q = q_ref[:, hs]
        qm.append(jnp.where(lane < HEAD_DIM, q, jnp.zeros_like(q)))
        qm.append(jnp.where(lane >= HEAD_DIM, q, jnp.zeros_like(q)))
    acc[...] = jnp.zeros_like(acc)

    def chunk(c, carry, visible):
        kc = k_ref[pl.ds(pl.multiple_of(c * tq, tq), tq), :]
        vtc = [vt[h, c] for h in range(hps)]
        old = [acc[i] for i in range(len(qm))]
        n = len(qm)
        ms, ls = carry[0::2], carry[1::2]
        ss = [lax.dot_general(kc[:, heads[i // 2]], qm[i], _NT, preferred_element_type=F32)
              for i in range(n)]
        if visible is not None:
            ss = [jnp.where(visible, s, -jnp.inf) for s in ss]
        m_new = [jnp.maximum(ms[i], jnp.max(ss[i], axis=0, keepdims=True)) for i in range(n)]
        ps = [jnp.exp(ss[i] - m_new[i]) for i in range(n)]
        al = [jnp.exp(ms[i] - m_new[i]) for i in range(n)]
        l_new = [al[i] * ls[i] + jnp.sum(ps[i], axis=0, keepdims=True) for i in range(n)]
        pv = [_dot(vtc[i // 2], ps[i].astype(BF16)) for i in range(n)]
        for i in range(n):
            acc[i] = al[i] * old[i] + pv[i]
        out = []
        for i in range(n):
            out += [m_new[i], l_new[i]]
        return tuple(out)

    neg = jnp.full((1, tq), -jnp.inf, F32)
    zero = jnp.zeros((1, tq), F32)
    carry = lax.fori_loop(0, qi, lambda c, cr: chunk(c, cr, None), (neg, zero) * len(qm))
    diag = (lax.broadcasted_iota(jnp.int32, (tq, tq), 0)
            <= lax.broadcasted_iota(jnp.int32, (tq, tq), 1))
    carry = chunk(qi, carry, diag)
    lam_full = _lambda_full(lam_ref, lam_init)
    for h, hs in enumerate(heads):
        l1, l2 = carry[4 * h + 1], carry[4 * h + 3]
        o = acc[2 * h] / l1 - lam_full * (acc[2 * h + 1] / l2)
        r = o * lax.rsqrt(jnp.mean(jnp.square(o), axis=0, keepdims=True) + LN_EPS) * g_ref[...]
        o_ref[:, hs] = (r * (1.0 - lam_init)).T.astype(BF16)


def _diff_attn_prompt(q, k, v, lam_l, sub_g_col, n, t, lam_init, tq=256, heads_per_step=4):
    m, aw = q.shape
    nq = t // tq
    wb = heads_per_step * HEAD_W
    return pl.pallas_call(
        functools.partial(_dattn_kernel, tq=tq, lam_init=lam_init),
        grid=(n, aw // wb, nq),
        in_specs=[pl.BlockSpec(lam_l.shape, lambda b, h, i: (0, 0)),
                  pl.BlockSpec((HEAD_W, 1), lambda b, h, i: (0, 0)),
                  pl.BlockSpec((tq, wb), lambda b, h, i: (b * nq + i, h)),
                  pl.BlockSpec((t, wb), lambda b, h, i: (b, h)),
                  pl.BlockSpec((t, wb), lambda b, h, i: (b, h))],
        out_specs=pl.BlockSpec((tq, wb), lambda b, h, i: (b * nq + i, h)),
        out_shape=jax.ShapeDtypeStruct((m, aw), BF16),
        scratch_shapes=[pltpu.VMEM((heads_per_step, nq, HEAD_W, tq), BF16),
                        pltpu.VMEM((2 * heads_per_step, HEAD_W, tq), F32)],
        compiler_params=_params("arbitrary", "arbitrary", "arbitrary"),
        name="diff_attn_prompt",
    )(lam_l, sub_g_col, q, k, v)


def _sattn_kernel(pt_ref, lam_ref, g_ref, q_ref, kn_ref, vn_ref, *rest,
                  pages_per_step, lam_init):
    del pt_ref
    k_refs = rest[:pages_per_step]
    v_refs = rest[pages_per_step:2 * pages_per_step]
    o_ref, qbd, knp, vnp, m_s, l_s, acc_s = rest[2 * pages_per_step:]
    p_id = pl.program_id(1)
    t_new, aw = q_ref.shape[1], q_ref.shape[2]
    nh = aw // HEAD_W
    rows = 2 * t_new
    page = knp.shape[0]

    def update(k_head, v_head, mask):
        s = jnp.concatenate(
            [lax.dot_general(qbd[h * rows:(h + 1) * rows, :], k_head(h),
                             _NT, preferred_element_type=F32) for h in range(nh)], axis=0)
        if mask is not None:
            s = jnp.where(mask, s, -jnp.inf)
        m_old = m_s[...]
        m_new = jnp.maximum(m_old, jnp.max(s, axis=-1, keepdims=True))
        p = jnp.exp(s - m_new)
        a = jnp.exp(m_old - m_new)
        l_s[...] = a * l_s[...] + jnp.sum(p, axis=-1, keepdims=True)
        p = p.astype(BF16)
        pv = jnp.concatenate(
            [_dot(p[h * rows:(h + 1) * rows, :], v_head(h)) for h in range(nh)], axis=0)
        acc_s[...] = a * acc_s[...] + pv
        m_s[...] = m_new

    @pl.when(p_id == 0)
    def _():
        lane = lax.broadcasted_iota(jnp.int32, (t_new, HEAD_W), 1)
        for h in range(nh):
            qh = q_ref[0, :, h * HEAD_W:(h + 1) * HEAD_W]
            qbd[h * rows:h * rows + t_new, :] = jnp.where(lane < HEAD_DIM, qh, jnp.zeros_like(qh))
            qbd[h * rows + t_new:(h + 1) * rows, :] = jnp.where(lane >= HEAD_DIM, qh, jnp.zeros_like(qh))
        knp[...] = jnp.zeros_like(knp)
        vnp[...] = jnp.zeros_like(vnp)
        knp[0:t_new, :] = kn_ref[0]
        vnp[0:t_new, :] = vn_ref[0]
        m_s[...] = jnp.full_like(m_s, -jnp.inf)
        l_s[...] = jnp.zeros_like(l_s)
        acc_s[...] = jnp.zeros_like(acc_s)
        r = lax.broadcasted_iota(jnp.int32, (nh * rows, page), 0)
        j = lax.broadcasted_iota(jnp.int32, (nh * rows, page), 1)
        update(lambda h: knp[:, h * HEAD_W:(h + 1) * HEAD_W],
               lambda h: vnp[:, h * HEAD_W:(h + 1) * HEAD_W], j <= r % t_new)

    def head_rows(refs, h):
        return jnp.concatenate([r[pl.ds(h, page, stride=nh), :] for r in refs], axis=0).astype(BF16)

    update(functools.partial(head_rows, k_refs), functools.partial(head_rows, v_refs), None)

    @pl.when(p_id == pl.num_programs(1) - 1)
    def _():
        o_all = acc_s[...] / l_s[...]
        lam_full = _lambda_full(lam_ref, lam_init)
        for h in range(nh):
            o1 = o_all[h * rows:h * rows + t_new, :]
            o2 = o_all[h * rows + t_new:(h + 1) * rows, :]
            o_ref[0, :, h * HEAD_W:(h + 1) * HEAD_W] = _sub_norm(
                o1 - lam_full * o2, g_ref[...], lam_init).astype(BF16)


def _diff_attn_sample(q, k_new, v_new, cache_k, cache_v, page_table, lam_l, sub_g, layer, lam_init,
                      pages_per_step=8):
    n, t_new, aw = q.shape
    nh = aw // HEAD_W
    page = cache_k.shape[2] // nh
    n_pages = page_table.shape[1]
    rows_all = nh * 2 * t_new

    def page_spec(i):
        return pl.BlockSpec((None, None, page * nh, HEAD_W),
                            lambda b, p, pt: (layer, pt[b, p * pages_per_step + i], 0, 0))

    tok_spec = pl.BlockSpec((1, t_new, aw), lambda b, p, pt: (b, 0, 0))
    grid_spec = pltpu.PrefetchScalarGridSpec(
        num_scalar_prefetch=1,
        grid=(n, n_pages // pages_per_step),
        in_specs=[pl.BlockSpec(lam_l.shape, lambda b, p, pt: (0, 0)),
                  pl.BlockSpec((1, HEAD_W), lambda b, p, pt: (0, 0)),
                  tok_spec, tok_spec, tok_spec]
                 + [page_spec(i) for i in range(pages_per_step)] * 2,
        out_specs=tok_spec,
        scratch_shapes=[pltpu.VMEM((rows_all, HEAD_W), BF16),
                        pltpu.VMEM((page, aw), BF16),
                        pltpu.VMEM((page, aw), BF16),
                        pltpu.VMEM((rows_all, 1), F32),
                        pltpu.VMEM((rows_all, 1), F32),
                        pltpu.VMEM((rows_all, HEAD_W), F32)],
    )
    return pl.pallas_call(
        functools.partial(_sattn_kernel, pages_per_step=pages_per_step, lam_init=lam_init),
        grid_spec=grid_spec,
        out_shape=jax.ShapeDtypeStruct((n, t_new, aw), BF16),
        compiler_params=_params("arbitrary", "arbitrary"),
        name="diff_attn_sample",
    )(page_table, lam_l, sub_g, q, k_new, v_new,
      *([cache_k] * pages_per_step), *([cache_v] * pages_per_step))


def _convb_kernel(u_ref, h_ref, w_ref, b_ref, g_ref, be_ref, o_ref, ext, conv_s, *, tt, rt):
    @pl.when(pl.program_id(1) == 0)
    def _():
        ext[0:CONV_PAD, :] = h_ref[0]
        ext[CONV_PAD:, :] = u_ref[...]

    base = pl.multiple_of(pl.program_id(1) * tt, SUBLANES)
    ch = o_ref.shape[1]
    for cb in range(ch // LANES):
        cs = slice(cb * LANES, (cb + 1) * LANES)

        def rows(r, _):
            r0 = pl.multiple_of(base + r * rt, SUBLANES)
            win = ext[pl.ds(r0, rt + CONV_PAD), cs]
            y = b_ref[:, cs] + win[CONV_PAD:CONV_PAD + rt] * w_ref[CONV_PAD:CONV_PAD + 1, cs]
            for s in range(SUBLANES):
                z = win[0:rt + SUBLANES] * w_ref[s:s + 1, cs]
                for a in range(1, CONV_PAD // SUBLANES):
                    o = a * SUBLANES + s
                    z = z + win[a * SUBLANES:a * SUBLANES + rt + SUBLANES] * w_ref[o:o + 1, cs]
                y = y + z[s:s + rt]
            conv_s[pl.ds(pl.multiple_of(r * rt, SUBLANES), rt), cs] = y
            return 0

        lax.fori_loop(0, tt // rt, rows, 0)
    y = _layer_norm(conv_s[...], g_ref[...], be_ref[...])
    o_ref[...] = (y * jax.nn.sigmoid(y)).astype(BF16)


def _conv_branch(u, hist, conv_w, conv_b, ln_g, ln_b, t, tt):
    m, ch = u.shape
    n = m // t
    nt = t // tt
    vec = pl.BlockSpec((1, ch), lambda b, i: (0, 0))
    return pl.pallas_call(
        functools.partial(_convb_kernel, tt=tt, rt=min(tt, 32)),
        grid=(n, nt),
        in_specs=[pl.BlockSpec((t, ch), lambda b, i: (b, 0)),
                  pl.BlockSpec((1, CONV_PAD, ch), lambda b, i: (b, 0, 0)),
                  pl.BlockSpec(conv_w.shape, lambda b, i: (0, 0)), vec, vec, vec],
        out_specs=pl.BlockSpec((tt, ch), lambda b, i: (b * nt + i, 0)),
        out_shape=jax.ShapeDtypeStruct((m, ch), BF16),
        scratch_shapes=[pltpu.VMEM((CONV_PAD + t, ch), F32), pltpu.VMEM((tt, ch), F32)],
        compiler_params=_params("arbitrary", "arbitrary"),
        name="conv_branch",
    )(u, hist, conv_w, conv_b, ln_g, ln_b)


def _mix_out_kernel(x_ref, a_ref, c_ref, ga_ref, gb_ref, wa_ref, wb_ref, wo_ref, g_ref, b_ref,
                    o_ref, merged, *, tn):
    att, cbr = a_ref[...], c_ref[...]
    for j in range(o_ref.shape[1] // tn):
        cs = slice(j * tn, (j + 1) * tn)
        merged[:, cs] = (ga_ref[:, cs] * _dot(att, wa_ref[:, cs])
                         + gb_ref[:, cs] * _dot(cbr, wb_ref[:, cs])).astype(BF16)
    o_ref[...] = _layer_norm(ALPHA * x_ref[...] + _dot(merged[...], wo_ref[...]),
                             g_ref[...], b_ref[...])


def _resident(shape):
    return pl.BlockSpec(shape, lambda i: (0,) * len(shape), pipeline_mode=pl.Buffered(1))


def _mix_out(x, att, cbr, ga, gb, w_a, w_b, w_o, g, b, tm, tn=512):
    m, d = x.shape
    kw = att.shape[1]
    wide = pl.BlockSpec((tm, d), lambda i: (i, 0))
    narrow = pl.BlockSpec((tm, kw), lambda i: (i, 0))
    return pl.pallas_call(
        functools.partial(_mix_out_kernel, tn=tn),
        grid=(m // tm,),
        in_specs=[wide, narrow, narrow, wide, wide, _resident(w_a.shape), _resident(w_b.shape),
                  _resident(w_o.shape), _resident(g.shape), _resident(b.shape)],
        out_specs=wide,
        out_shape=jax.ShapeDtypeStruct((m, d), F32),
        scratch_shapes=[pltpu.VMEM((tm, d), BF16)],
        compiler_params=_params("arbitrary"),
        name="mix_out",
    )(x, att, cbr, ga, gb, w_a, w_b, w_o, g, b)


def _proj_ln_kernel(x_ref, y_ref, w_ref, g_ref, b_ref, o_ref):
    o_ref[...] = _layer_norm(ALPHA * x_ref[...] + _dot(y_ref[...], w_ref[...]),
                             g_ref[...], b_ref[...])


def _proj_ln(x, y, w, g, b, tm):
    m, d = x.shape
    kw = y.shape[1]
    return pl.pallas_call(
        _proj_ln_kernel,
        grid=(m // tm,),
        in_specs=[pl.BlockSpec((tm, d), lambda i: (i, 0)),
                  pl.BlockSpec((tm, kw), lambda i: (i, 0)),
                  _resident(w.shape), _resident(g.shape), _resident(b.shape)],
        out_specs=pl.BlockSpec((tm, d), lambda i: (i, 0)),
        out_shape=jax.ShapeDtypeStruct((m, d), F32),
        compiler_params=_params("arbitrary"),
        name="proj_ln",
    )(x, y, w, g, b)


def _proj_kernel(x_ref, w_ref, o_ref):
    o_ref[...] = _dot(x_ref[...].astype(BF16), w_ref[...]).astype(o_ref.dtype)


def _proj(x, w, out_dtype, tm):
    m, d = x.shape
    nw = w.shape[1]
    return pl.pallas_call(
        _proj_kernel,
        grid=(m // tm,),
        in_specs=[pl.BlockSpec((tm, d), lambda i: (i, 0)), _resident(w.shape)],
        out_specs=pl.BlockSpec((tm, nw), lambda i: (i, 0)),
        out_shape=jax.ShapeDtypeStruct((m, nw), out_dtype),
        compiler_params=_params("arbitrary"),
        name="proj",
    )(x, w)


def _xattn_head(q, kb, vb, cs):
    s = lax.dot_general(q, kb[:, cs], _NT, preferred_element_type=F32) * q.shape[1] ** -0.5
    e = jnp.exp(s - jnp.max(s, axis=-1, keepdims=True))
    p = e / jnp.sum(e, axis=-1, keepdims=True)
    return _dot(p.astype(BF16), vb[:, cs]).astype(BF16)


def _xattn_kernel(q_ref, mk_ref, mv_ref, o_ref, kb, vb):
    @pl.when(pl.program_id(1) == 0)
    def _():
        kb[...] = mk_ref[...].astype(BF16)
        vb[...] = mv_ref[...].astype(BF16)

    hd = q_ref.shape[1] // X_HEADS
    for h in range(X_HEADS):
        cs = slice(h * hd, (h + 1) * hd)
        o_ref[:, cs] = _xattn_head(q_ref[:, cs], kb, vb, cs)


def _xattn_ln_kernel(x_ref, mk_ref, mv_ref, wq_ref, wo_ref, g_ref, b_ref, o_ref, kb, vb, att,
                     *, tiles_per_seq):
    @pl.when(pl.program_id(0) % tiles_per_seq == 0)
    def _():
        kb[...] = mk_ref[...].astype(BF16)
        vb[...] = mv_ref[...].astype(BF16)

    x = x_ref[...]
    xb = x.astype(BF16)
    hd = x.shape[1] // X_HEADS
    for h in range(X_HEADS):
        cs = slice(h * hd, (h + 1) * hd)
        att[:, cs] = _xattn_head(_dot(xb, wq_ref[:, cs]).astype(BF16), kb, vb, cs)
    o_ref[...] = _layer_norm(ALPHA * x + _dot(att[...], wo_ref[...]), g_ref[...], b_ref[...])


def _cross_attn_ln(x, mk, mv, w_q, w_o, g, b, t, n_mem, tm):
    m, d = x.shape
    tiles_per_seq = t // tm
    mem = pl.BlockSpec((n_mem, d), lambda i: (i // tiles_per_seq, 0))
    tile = pl.BlockSpec((tm, d), lambda i: (i, 0))
    return pl.pallas_call(
        functools.partial(_xattn_ln_kernel, tiles_per_seq=tiles_per_seq),
        grid=(m // tm,),
        in_specs=[tile, mem, mem, _resident(w_q.shape), _resident(w_o.shape),
                  _resident(g.shape), _resident(b.shape)],
        out_specs=tile,
        out_shape=jax.ShapeDtypeStruct((m, d), F32),
        scratch_shapes=[pltpu.VMEM((n_mem, d), BF16), pltpu.VMEM((n_mem, d), BF16),
                        pltpu.VMEM((tm, d), BF16)],
        compiler_params=_params("arbitrary"),
        name="cross_attn_ln",
    )(x, mk, mv, w_q, w_o, g, b)


def _cross_attn(q, mk, mv, n, t, n_mem, mem_block0, tq):
    m, d = q.shape
    nq = t // tq
    mem = pl.BlockSpec((n_mem, d), lambda b, i: (mem_block0 + b, 0))
    return pl.pallas_call(
        _xattn_kernel,
        grid=(n, nq),
        in_specs=[pl.BlockSpec((tq, d), lambda b, i: (b * nq + i, 0)), mem, mem],
        out_specs=pl.BlockSpec((tq, d), lambda b, i: (b * nq + i, 0)),
        out_shape=jax.ShapeDtypeStruct((m, d), BF16),
        scratch_shapes=[pltpu.VMEM((n_mem, d), BF16), pltpu.VMEM((n_mem, d), BF16)],
        compiler_params=_params("arbitrary", "arbitrary"),
        name="cross_attn",
    )(q, mk, mv)


def _ffn_kernel(*refs, seg, tiles_per_seq, carried, tail_rows):
    if carried:
        (x_ref, wa_ref, wg_ref, wd_ref, cw_ref, cb_ref, g_ref, b_ref,
         o_ref, tail_ref, xb, acc, carry) = refs
    else:
        (x_ref, wa_ref, wg_ref, wd_ref, cw_ref, cb_ref, g_ref, b_ref, s0_ref, s1_ref,
         o_ref, tail_ref, xb, acc) = refs
    i, j = pl.program_id(0), pl.program_id(1)
    tm, tf = x_ref.shape[0], wa_ref.shape[1]

    @pl.when(j == 0)
    def _():
        xb[...] = x_ref[...].astype(BF16)
        acc[...] = jnp.zeros_like(acc)

    x = xb[...]
    a = _dot(x, wa_ref[...])
    gate = _dot(x, wg_ref[...])
    if carried:
        @pl.when(i == 0)
        def _():
            carry[j] = jnp.zeros((SUBLANES, tf), F32)

        prev = jnp.where(i % tiles_per_seq != 0, carry[j], 0.0)
        s0 = prev[SUBLANES - 2:SUBLANES - 1]
        s1 = prev[SUBLANES - 1:SUBLANES]
        carry[j] = a[tm - SUBLANES:]
    else:
        s0, s1 = s0_ref[...], s1_ref[...]
    t = lax.broadcasted_iota(jnp.int32, (tm, tf), 0) % seg
    p1 = jnp.where(t == 0, s1, pltpu.roll(a, 1, axis=0))
    p2 = jnp.where(t == 0, s0, jnp.where(t == 1, s1, pltpu.roll(a, 2, axis=0)))
    cw = cw_ref[...]
    a_c = cw[0:1] * p2 + cw[1:2] * p1 + cw[2:3] * a + cb_ref[...]
    y = 0.5 * a_c * (1.0 + lax.erf(a_c * math.sqrt(0.5))) * gate
    acc[...] += _dot(y.astype(BF16), wd_ref[...])
    tail_ref[...] = a[tm - tail_rows:]

    @pl.when(j == pl.num_programs(1) - 1)
    def _():
        o_ref[...] = _layer_norm(ALPHA * x_ref[...] + acc[...], g_ref[...], b_ref[...])


def _conv_ffn(x, w_up, cw, cb, w_down, g, b, seg, tm, state_rows=None, tf=512):
    m, d = x.shape
    f = w_down.shape[0]
    nj = f // tf
    carried = state_rows is None
    tail_rows = SUBLANES if carried else tm
    tiles_per_seq = max(seg // tm, 1)
    vec_d = pl.BlockSpec((1, d), lambda i, j: (0, 0))
    in_specs = [pl.BlockSpec((tm, d), lambda i, j: (i, 0)),
                pl.BlockSpec((d, tf), lambda i, j: (0, j)),
                pl.BlockSpec((d, tf), lambda i, j: (0, nj + j)),
                pl.BlockSpec((tf, d), lambda i, j: (j, 0)),
                pl.BlockSpec((FFN_CONV_K, tf), lambda i, j: (0, j)),
                pl.BlockSpec((1, tf), lambda i, j: (0, j)),
                vec_d, vec_d]
    args = [x, w_up, w_up, w_down, cw, cb, g, b]
    scratch = [pltpu.VMEM((tm, d), BF16), pltpu.VMEM((tm, d), F32)]
    if carried:
        scratch.append(pltpu.VMEM((nj, SUBLANES, tf), F32))
    else:
        in_specs += [pl.BlockSpec((tm, tf), lambda i, j: (i, j))] * 2
        args += list(state_rows)
    return pl.pallas_call(
        functools.partial(_ffn_kernel, seg=seg, tiles_per_seq=tiles_per_seq, carried=carried,
                          tail_rows=tail_rows),
        grid=(m // tm, nj),
        in_specs=in_specs,
        out_specs=[pl.BlockSpec((tm, d), lambda i, j: (i, 0)),
                   pl.BlockSpec((tail_rows, tf), lambda i, j: (i, j))],
        out_shape=[jax.ShapeDtypeStruct((m, d), F32),
                   jax.ShapeDtypeStruct((m // tm * tail_rows, f), F32)],
        scratch_shapes=scratch,
        compiler_params=_params("arbitrary", "arbitrary"),
        name="conv_ffn",
    )(*args)


def _row(v):
    return v.reshape(1, -1)


def _mixing(x, t, w, hist, attend, tm):
    q, k, v, kb, vb, u, ga, gb = _in_proj(x, w["w_in"], tm)
    att = attend(q, kb, vb)
    lead = CONV_PAD - CONV_STATE
    hist = jnp.pad(hist, ((0, 0), (lead, 0), (0, 0)))
    conv_w = jnp.pad(w["conv_w"], ((lead, CONV_PAD + SUBLANES - lead - CONV_K), (0, 0)))
    cbr = _conv_branch(u, hist, conv_w, w["conv_b"], w["conv_ln_g"], w["conv_ln_b"], t,
                       tt=min(t, 256))
    x = _mix_out(x, att, cbr, ga, gb, w["w_a"], w["w_b"], w["w_o"], w["ln1_g"], w["ln1_b"],
                 min(tm, 256))
    return x, k, v, u


def kernel(x_prompt, x_sample, cache_k, cache_v, cache_mem_k, cache_mem_v, state_conv, state_ffn,
           page_table, mem_prompt, w_in, lam, subln_g, w_a, conv_w, conv_b, conv_ln_g, conv_ln_b,
           w_b, w_o, ln1_g, ln1_b, w_xq, w_xk, w_xv, w_xo, ln2_g, ln2_b, w_up, ffn_conv_w,
           ffn_conv_b, w_down, ln3_g, ln3_b):
    n_p, t_p, d = x_prompt.shape
    n_s, t_s, _ = x_sample.shape
    depth = w_in.shape[0]
    aw = d // 2
    nh = aw // HEAD_W
    n_mem = mem_prompt.shape[1]
    f = w_down.shape[1]
    page = cache_k.shape[2]
    m_p, m_s = n_p * t_p, n_s * t_s
    tm_p, tm_s = min(512, m_p), m_s

    cache_k2 = cache_k.reshape(depth, -1, page * nh, HEAD_W)
    cache_v2 = cache_v.reshape(depth, -1, page * nh, HEAD_W)
    mem_k_s = cache_mem_k.reshape(depth * n_s * n_mem, d)
    mem_v_s = cache_mem_v.reshape(depth * n_s * n_mem, d)
    mem2 = mem_prompt.reshape(n_p * n_mem, d)

    xp = x_prompt.reshape(m_p, d)
    xs = x_sample.reshape(m_s, d)
    outs = {name: [] for name in ("kp", "vp", "mkp", "mvp", "cp", "fp", "ks", "vs", "cs", "fs")}
    for l in range(depth):
        lam_init = 0.8 - 0.6 * math.exp(-0.3 * l)
        w = dict(w_in=w_in[l].astype(BF16), w_a=w_a[l].astype(BF16), w_b=w_b[l].astype(BF16),
                 w_o=w_o[l].astype(BF16), w_xq=w_xq[l].astype(BF16), w_xk=w_xk[l].astype(BF16),
                 w_xv=w_xv[l].astype(BF16), w_xo=w_xo[l].astype(BF16), w_up=w_up[l].astype(BF16),
                 w_down=w_down[l].astype(BF16), conv_w=conv_w[l], conv_b=_row(conv_b[l]),
                 conv_ln_g=_row(conv_ln_g[l]), conv_ln_b=_row(conv_ln_b[l]),
                 ln1_g=_row(ln1_g[l]), ln1_b=_row(ln1_b[l]), ln2_g=_row(ln2_g[l]),
                 ln2_b=_row(ln2_b[l]), ln3_g=_row(ln3_g[l]), ln3_b=_row(ln3_b[l]),
                 ffn_conv_w=ffn_conv_w[l], ffn_conv_b=_row(ffn_conv_b[l]))

        mk = _proj(mem2, w["w_xk"], F32, min(512, mem2.shape[0]))
        mv = _proj(mem2, w["w_xv"], F32, min(512, mem2.shape[0]))
        attend_p = lambda q, kb, vb: _diff_attn_prompt(
            q, kb, vb, lam[l], subln_g[l].reshape(-1, 1), n_p, t_p, lam_init)
        xp, k, v, u = _mixing(xp, t_p, w, jnp.zeros((n_p, CONV_STATE, aw), F32), attend_p, tm_p)
        xp = _cross_attn_ln(xp, mk, mv, w["w_xq"], w["w_xo"], w["ln2_g"], w["ln2_b"], t_p, n_mem,
                            min(256, t_p))
        xp, tail = _conv_ffn(xp, w["w_up"], w["ffn_conv_w"], w["ffn_conv_b"], w["w_down"],
                             w["ln3_g"], w["ln3_b"], seg=t_p, tm=tm_p)
        outs["kp"].append(k.reshape(n_p, t_p, nh, HEAD_W))
        outs["vp"].append(v.reshape(n_p, t_p, nh, HEAD_W))
        outs["mkp"].append(mk.reshape(n_p, n_mem, X_HEADS, d // X_HEADS))
        outs["mvp"].append(mv.reshape(n_p, n_mem, X_HEADS, d // X_HEADS))
        outs["cp"].append(u.reshape(n_p, t_p, aw)[:, -CONV_STATE:])
        outs["fp"].append(tail.reshape(n_p, t_p // tm_p, SUBLANES, f)[:, -1, -(FFN_CONV_K - 1):])

        def attend_s(q, kb, vb):
            att = _diff_attn_sample(q.reshape(n_s, t_s, aw), kb.reshape(n_s, t_s, aw),
                                    vb.reshape(n_s, t_s, aw), cache_k2, cache_v2, page_table,
                                    lam[l], _row(subln_g[l]), l, lam_init)
            return att.reshape(m_s, aw)

        xs, k, v, u = _mixing(xs, t_s, w, state_conv[l], attend_s, tm_s)
        xq = _proj(xs, w["w_xq"], BF16, tm_s)
        xa = _cross_attn(xq, mem_k_s, mem_v_s, n_s, t_s, n_mem, l * n_s, t_s)
        xs = _proj_ln(xs, xa, w["w_xo"], w["ln2_g"], w["ln2_b"], tm_s)
        st = state_ffn[l]
        s0 = jnp.repeat(st[:, 0], t_s, axis=0)
        s1 = jnp.repeat(st[:, 1], t_s, axis=0)
        xs, a_s = _conv_ffn(xs, w["w_up"], w["ffn_conv_w"], w["ffn_conv_b"], w["w_down"],
                            w["ln3_g"], w["ln3_b"], seg=t_s, tm=tm_s, state_rows=(s0, s1))
        outs["ks"].append(k.reshape(n_s, t_s, nh, HEAD_W))
        outs["vs"].append(v.reshape(n_s, t_s, nh, HEAD_W))
        u_ext = jnp.concatenate([state_conv[l], u.reshape(n_s, t_s, aw)], axis=1)
        outs["cs"].append(u_ext[:, -CONV_STATE:])
        a_ext = jnp.concatenate([st, a_s.reshape(n_s, t_s, f)], axis=1)
        outs["fs"].append(a_ext[:, -(FFN_CONV_K - 1):])

    stk = {name: jnp.stack(vals) for name, vals in outs.items()}
    return (xp.reshape(n_p, t_p, d), xs.reshape(n_s, t_s, d),
            stk["kp"], stk["vp"], stk["mkp"], stk["mvp"], stk["cp"], stk["fp"],
            stk["ks"], stk["vs"], stk["cs"], stk["fs"])
```

```python
import functools
import math
from typing import NamedTuple

import jax
import jax.numpy as jnp
from jax import lax
from jax.experimental import pallas as pl
from jax.experimental.pallas import tpu as pltpu

F32 = jnp.float32
BF16 = jnp.bfloat16

DEPTH = 4
HEAD_DIM = 64
HEAD_W = 2 * HEAD_DIM
CONV_K = 31
CONV_STATE = CONV_K - 1
CONV_PAD = 32
FFN_CONV_K = 3
X_HEADS = 4
ALPHA = (2.0 * DEPTH) ** 0.25
LN_EPS = 1e-5
QK_SCALE = HEAD_DIM ** -0.5

V7X_VMEM_LIMIT_BYTES = 56 * 1024 * 1024
SUBLANES = 8
LANES = 128

_NT = (((1,), (1,)), ((), ()))


def _params(*semantics):
    return pltpu.CompilerParams(dimension_semantics=semantics,
                                vmem_limit_bytes=V7X_VMEM_LIMIT_BYTES)


class _LayerWeight(NamedTuple):
    stack: jax.Array
    layer: int

    @property
    def shape(self):
        return self.stack.shape[1:]

    def spec(self, block, index_map, **kwargs):
        return pl.BlockSpec((None,) + tuple(block), lambda *g: (self.layer,) + tuple(index_map(*g)),
                            **kwargs)

    def resident(self):
        return self.spec(self.shape, lambda *g: (0, 0), pipeline_mode=pl.Buffered(1))


def _layer_norm(y, g, b):
    mu = jnp.mean(y, axis=-1, keepdims=True)
    var = jnp.mean(jnp.square(y - mu), axis=-1, keepdims=True)
    return (y - mu) * lax.rsqrt(var + LN_EPS) * g + b


def _dot(a, b):
    return jnp.dot(a, b, preferred_element_type=F32)


def _lambda_full(lam_ref, lam_init):
    lf = lam_ref[...]
    s1 = jnp.sum(lf[0:1] * lf[1:2], axis=-1, keepdims=True)
    s2 = jnp.sum(lf[2:3] * lf[3:4], axis=-1, keepdims=True)
    return jnp.exp(s1) - jnp.exp(s2) + lam_init


def _sub_norm(o, g, lam_init):
    r = o * lax.rsqrt(jnp.mean(jnp.square(o), axis=-1, keepdims=True) + LN_EPS) * g
    return r * (1.0 - lam_init)


def _inproj_kernel(x_ref, wq, wk, wv, wua, wub, wga, wgb,
                   q_o, k_o, v_o, kb_o, vb_o, u_o, ga_o, gb_o, xb):
    @pl.when(pl.program_id(1) == 0)
    def _():
        xb[...] = x_ref[...].astype(BF16)

    x = xb[...]
    q_o[...] = (_dot(x, wq[...]) * QK_SCALE).astype(BF16)
    k = _dot(x, wk[...])
    k_o[...] = k
    kb_o[...] = k.astype(BF16)
    v = _dot(x, wv[...])
    v_o[...] = v
    vb_o[...] = v.astype(BF16)
    u_o[...] = _dot(x, wua[...]) * jax.nn.sigmoid(_dot(x, wub[...]))
    ga_o[...] = jax.nn.sigmoid(_dot(x, wga[...])).astype(BF16)
    gb_o[...] = jax.nn.sigmoid(_dot(x, wgb[...])).astype(BF16)


def _in_proj(x, w_in, tm, tn=256):
    m, d = x.shape
    aw = d // 2
    nj = aw // tn
    tg = 2 * tn
    offs = (0, aw, 2 * aw, 3 * aw, 4 * aw)
    w_specs = [w_in.spec((d, tn), functools.partial(lambda i, j, o: (0, o + j), o=o // tn))
               for o in offs]
    w_specs += [w_in.spec((d, tg), functools.partial(lambda i, j, o: (0, o + j), o=o // tg))
                for o in (5 * aw, 5 * aw + d)]
    narrow = pl.BlockSpec((tm, tn), lambda i, j: (i, j))
    wide = pl.BlockSpec((tm, tg), lambda i, j: (i, j))
    return pl.pallas_call(
        _inproj_kernel,
        grid=(m // tm, nj),
        in_specs=[pl.BlockSpec((tm, d), lambda i, j: (i, 0))] + w_specs,
        out_specs=[narrow] * 6 + [wide] * 2,
        out_shape=[jax.ShapeDtypeStruct((m, aw), BF16),
                   jax.ShapeDtypeStruct((m, aw), F32),
                   jax.ShapeDtypeStruct((m, aw), F32),
                   jax.ShapeDtypeStruct((m, aw), BF16),
                   jax.ShapeDtypeStruct((m, aw), BF16),
                   jax.ShapeDtypeStruct((m, aw), F32),
                   jax.ShapeDtypeStruct((m, d), BF16),
                   jax.ShapeDtypeStruct((m, d), BF16)],
        scratch_shapes=[pltpu.VMEM((tm, d), BF16)],
        compiler_params=_params("arbitrary", "arbitrary"),
        name="in_proj",
    )(x, *([w_in.stack] * 7))


def _dattn_kernel(lam_ref, g_ref, q_ref, k_ref, v_ref, o_ref, vt, acc, *, tq, lam_init):
    qi = pl.program_id(2)
    hps = q_ref.shape[1] // HEAD_W
    heads = [slice(h * HEAD_W, (h + 1) * HEAD_W) for h in range(hps)]

    @pl.when(qi == 0)
    def _():
        for c in range(vt.shape[1]):
            for h, hs in enumerate(heads):
                vt[h, c] = v_ref[c * tq:(c + 1) * tq, hs].astype(F32).T.astype(BF16)

    lane = lax.broadcasted_iota(jnp.int32, (tq, HEAD_W), 1)
    qm = []
    for hs in heads:
        q = q_ref[:, hs]
        qm.append(jnp.where(lane < HEAD_DIM, q, jnp.zeros_like(q)))
        qm.append(jnp.where(lane >= HEAD_DIM, q, jnp.zeros_like(q)))
    acc[...] = jnp.zeros_like(acc)

    def chunk(c, carry, visible):
        kc = k_ref[pl.ds(pl.multiple_of(c * tq, tq), tq), :]
        vtc = [vt[h, c] for h in range(hps)]
        old = [acc[i] for i in range(len(qm))]
        n = len(qm)
        ms, ls = carry[0::2], carry[1::2]
        ss = [lax.dot_general(kc[:, heads[i // 2]], qm[i], _NT, preferred_element_type=F32)
              for i in range(n)]
        if visible is not None:
            ss = [jnp.where(visible, s, -jnp.inf) for s in ss]
        m_new = [jnp.maximum(ms[i], jnp.max(ss[i], axis=0, keepdims=True)) for i in range(n)]
        ps = [jnp.exp(ss[i] - m_new[i]) for i in range(n)]
        al = [jnp.exp(ms[i] - m_new[i]) for i in range(n)]
        l_new = [al[i] * ls[i] + jnp.sum(ps[i], axis=0, keepdims=True) for i in range(n)]
        pv = [_dot(vtc[i // 2], ps[i].astype(BF16)) for i in range(n)]
        for i in range(n):
            acc[i] = al[i] * old[i] + pv[i]
        out = []
        for i in range(n):
            out += [m_new[i], l_new[i]]
        return tuple(out)

    neg = jnp.full((1, tq), -jnp.inf, F32)
    zero = jnp.zeros((1, tq), F32)
    carry = lax.fori_loop(0, qi, lambda c, cr: chunk(c, cr, None), (neg, zero) * len(qm))
    diag = (lax.broadcasted_iota(jnp.int32, (tq, tq), 0)
            <= lax.broadcasted_iota(jnp.int32, (tq, tq), 1))
    carry = chunk(qi, carry, diag)
    lam_full = _lambda_full(lam_ref, lam_init)
    for h, hs in enumerate(heads):
        l1, l2 = carry[4 * h + 1], carry[4 * h + 3]
        o = acc[2 * h] / l1 - lam_full * (acc[2 * h + 1] / l2)
        r = o * lax.rsqrt(jnp.mean(jnp.square(o), axis=0, keepdims=True) + LN_EPS) * g_ref[...]
        o_ref[:, hs] = (r * (1.0 - lam_init)).T.astype(BF16)


def _diff_attn_prompt(q, k, v, lam_l, sub_g_col, n, t, lam_init, tq=256, heads_per_step=4):
    m, aw = q.shape
    nq = t // tq
    wb = heads_per_step * HEAD_W
    return pl.pallas_call(
        functools.partial(_dattn_kernel, tq=tq, lam_init=lam_init),
        grid=(n, aw // wb, nq),
        in_specs=[pl.BlockSpec(lam_l.shape, lambda b, h, i: (0, 0)),
                  pl.BlockSpec((HEAD_W, 1), lambda b, h, i: (0, 0)),
                  pl.BlockSpec((tq, wb), lambda b, h, i: (b * nq + i, h)),
                  pl.BlockSpec((t, wb), lambda b, h, i: (b, h)),
                  pl.BlockSpec((t, wb), lambda b, h, i: (b, h))],
        out_specs=pl.BlockSpec((tq, wb), lambda b, h, i: (b * nq + i, h)),
        out_shape=jax.ShapeDtypeStruct((m, aw), BF16),
        scratch_shapes=[pltpu.VMEM((heads_per_step, nq, HEAD_W, tq), BF16),
                        pltpu.VMEM((2 * heads_per_step, HEAD_W, tq), F32)],
        compiler_params=_params("arbitrary", "arbitrary", "arbitrary"),
        name="diff_attn_prompt",
    )(lam_l, sub_g_col, q, k, v)


def _sattn_kernel(pt_ref, lam_ref, g_ref, q_ref, kn_ref, vn_ref, *rest,
                  pages_per_step, lam_init):
    del pt_ref
    k_refs = rest[:pages_per_step]
    v_refs = rest[pages_per_step:2 * pages_per_step]
    o_ref, qbd, knp, vnp, m_s, l_s, acc_s = rest[2 * pages_per_step:]
    p_id = pl.program_id(1)
    t_new, aw = q_ref.shape[1], q_ref.shape[2]
    nh = aw // HEAD_W
    rows = 2 * t_new
    page = knp.shape[0]

    def update(k_head, v_head, mask):
        s = jnp.concatenate(
            [lax.dot_general(qbd[h * rows:(h + 1) * rows, :], k_head(h),
                             _NT, preferred_element_type=F32) for h in range(nh)], axis=0)
        if mask is not None:
            s = jnp.where(mask, s, -jnp.inf)
        m_old = m_s[...]
        m_new = jnp.maximum(m_old, jnp.max(s, axis=-1, keepdims=True))
        p = jnp.exp(s - m_new)
        a = jnp.exp(m_old - m_new)
        l_s[...] = a * l_s[...] + jnp.sum(p, axis=-1, keepdims=True)
        p = p.astype(BF16)
        pv = jnp.concatenate(
            [_dot(p[h * rows:(h + 1) * rows, :], v_head(h)) for h in range(nh)], axis=0)
        acc_s[...] = a * acc_s[...] + pv
        m_s[...] = m_new

    @pl.when(p_id == 0)
    def _():
        lane = lax.broadcasted_iota(jnp.int32, (t_new, HEAD_W), 1)
        for h in range(nh):
            qh = q_ref[0, :, h * HEAD_W:(h + 1) * HEAD_W]
            qbd[h * rows:h * rows + t_new, :] = jnp.where(lane < HEAD_DIM, qh, jnp.zeros_like(qh))
            qbd[h * rows + t_new:(h + 1) * rows, :] = jnp.where(lane >= HEAD_DIM, qh, jnp.zeros_like(qh))
        knp[...] = jnp.zeros_like(knp)
        vnp[...] = jnp.zeros_like(vnp)
        knp[0:t_new, :] = kn_ref[0]
        vnp[0:t_new, :] = vn_ref[0]
        m_s[...] = jnp.full_like(m_s, -jnp.inf)
        l_s[...] = jnp.zeros_like(l_s)
        acc_s[...] = jnp.zeros_like(acc_s)
        r = lax.broadcasted_iota(jnp.int32, (nh * rows, page), 0)
        j = lax.broadcasted_iota(jnp.int32, (nh * rows, page), 1)
        update(lambda h: knp[:, h * HEAD_W:(h + 1) * HEAD_W],
               lambda h: vnp[:, h * HEAD_W:(h + 1) * HEAD_W], j <= r % t_new)

    def head_rows(refs, h):
        return jnp.concatenate([r[pl.ds(h, page, stride=nh), :] for r in refs], axis=0).astype(BF16)

    update(functools.partial(head_rows, k_refs), functools.partial(head_rows, v_refs), None)

    @pl.when(p_id == pl.num_programs(1) - 1)
    def _():
        o_all = acc_s[...] / l_s[...]
        lam_full = _lambda_full(lam_ref, lam_init)
        for h in range(nh):
            o1 = o_all[h * rows:h * rows + t_new, :]
            o2 = o_all[h * rows + t_new:(h + 1) * rows, :]
            o_ref[0, :, h * HEAD_W:(h + 1) * HEAD_W] = _sub_norm(
                o1 - lam_full * o2, g_ref[...], lam_init).astype(BF16)


def _diff_attn_sample(q, k_new, v_new, cache_k, cache_v, page_table, lam_l, sub_g, layer, lam_init,
                      pages_per_step=8):
    n, t_new, aw = q.shape
    nh = aw // HEAD_W
    page = cache_k.shape[2] // nh
    n_pages = page_table.shape[1]
    rows_all = nh * 2 * t_new

    def page_spec(i):
        return pl.BlockSpec((None, None, page * nh, HEAD_W),
                            lambda b, p, pt: (layer, pt[b, p * pages_per_step + i], 0, 0))

    tok_spec = pl.BlockSpec((1, t_new, aw), lambda b, p, pt: (b, 0, 0))
    grid_spec = pltpu.PrefetchScalarGridSpec(
        num_scalar_prefetch=1,
        grid=(n, n_pages // pages_per_step),
        in_specs=[pl.BlockSpec(lam_l.shape, lambda b, p, pt: (0, 0)),
                  pl.BlockSpec((1, HEAD_W), lambda b, p, pt: (0, 0)),
                  tok_spec, tok_spec, tok_spec]
                 + [page_spec(i) for i in range(pages_per_step)] * 2,
        out_specs=tok_spec,
        scratch_shapes=[pltpu.VMEM((rows_all, HEAD_W), BF16),
                        pltpu.VMEM((page, aw), BF16),
                        pltpu.VMEM((page, aw), BF16),
                        pltpu.VMEM((rows_all, 1), F32),
                        pltpu.VMEM((rows_all, 1), F32),
                        pltpu.VMEM((rows_all, HEAD_W), F32)],
    )
    return pl.pallas_call(
        functools.partial(_sattn_kernel, pages_per_step=pages_per_step, lam_init=lam_init),
        grid_spec=grid_spec,
        out_shape=jax.ShapeDtypeStruct((n, t_new, aw), BF16),
        compiler_params=_params("arbitrary", "arbitrary"),
        name="diff_attn_sample",
    )(page_table, lam_l, sub_g, q, k_new, v_new,
      *([cache_k] * pages_per_step), *([cache_v] * pages_per_step))


def _convb_kernel(u_ref, h_ref, w_ref, b_ref, g_ref, be_ref, o_ref, ext, conv_s, *, tt, rt):
    @pl.when(pl.program_id(1) == 0)
    def _():
        ext[0:CONV_PAD, :] = h_ref[0]
        ext[CONV_PAD:, :] = u_ref[...]

    base = pl.multiple_of(pl.program_id(1) * tt, SUBLANES)
    ch = o_ref.shape[1]
    for cb in range(ch // LANES):
        cs = slice(cb * LANES, (cb + 1) * LANES)

        def rows(r, _):
            r0 = pl.multiple_of(base + r * rt, SUBLANES)
            win = ext[pl.ds(r0, rt + CONV_PAD), cs]
            y = b_ref[:, cs] + win[CONV_PAD:CONV_PAD + rt] * w_ref[CONV_PAD:CONV_PAD + 1, cs]
            for s in range(SUBLANES):
                z = win[0:rt + SUBLANES] * w_ref[s:s + 1, cs]
                for a in range(1, CONV_PAD // SUBLANES):
                    o = a * SUBLANES + s
                    z = z + win[a * SUBLANES:a * SUBLANES + rt + SUBLANES] * w_ref[o:o + 1, cs]
                y = y + z[s:s + rt]
            conv_s[pl.ds(pl.multiple_of(r * rt, SUBLANES), rt), cs] = y
            return 0

        lax.fori_loop(0, tt // rt, rows, 0)
    y = _layer_norm(conv_s[...], g_ref[...], be_ref[...])
    o_ref[...] = (y * jax.nn.sigmoid(y)).astype(BF16)


def _conv_branch(u, hist, conv_w, conv_b, ln_g, ln_b, t, tt):
    m, ch = u.shape
    n = m // t
    nt = t // tt
    vec = pl.BlockSpec((1, ch), lambda b, i: (0, 0))
    return pl.pallas_call(
        functools.partial(_convb_kernel, tt=tt, rt=min(tt, 32)),
        grid=(n, nt),
        in_specs=[pl.BlockSpec((t, ch), lambda b, i: (b, 0)),
                  pl.BlockSpec((1, CONV_PAD, ch), lambda b, i: (b, 0, 0)),
                  pl.BlockSpec(conv_w.shape, lambda b, i: (0, 0)), vec, vec, vec],
        out_specs=pl.BlockSpec((tt, ch), lambda b, i: (b * nt + i, 0)),
        out_shape=jax.ShapeDtypeStruct((m, ch), BF16),
        scratch_shapes=[pltpu.VMEM((CONV_PAD + t, ch), F32), pltpu.VMEM((tt, ch), F32)],
        compiler_params=_params("arbitrary", "arbitrary"),
        name="conv_branch",
    )(u, hist, conv_w, conv_b, ln_g, ln_b)


def _mix_out_kernel(x_ref, a_ref, c_ref, ga_ref, gb_ref, wa_ref, wb_ref, wo_ref, g_ref, b_ref,
                    o_ref, merged, *, tn):
    att, cbr = a_ref[...], c_ref[...]
    for j in range(o_ref.shape[1] // tn):
        cs = slice(j * tn, (j + 1) * tn)
        merged[:, cs] = (ga_ref[:, cs] * _dot(att, wa_ref[:, cs])
                         + gb_ref[:, cs] * _dot(cbr, wb_ref[:, cs])).astype(BF16)
    o_ref[...] = _layer_norm(ALPHA * x_ref[...] + _dot(merged[...], wo_ref[...]),
                             g_ref[...], b_ref[...])


def _resident(shape):
    return pl.BlockSpec(shape, lambda i: (0,) * len(shape), pipeline_mode=pl.Buffered(1))


def _mix_out(x, att, cbr, ga, gb, w_a, w_b, w_o, g, b, tm, tn=512):
    m, d = x.shape
    kw = att.shape[1]
    wide = pl.BlockSpec((tm, d), lambda i: (i, 0))
    narrow = pl.BlockSpec((tm, kw), lambda i: (i, 0))
    return pl.pallas_call(
        functools.partial(_mix_out_kernel, tn=tn),
        grid=(m // tm,),
        in_specs=[wide, narrow, narrow, wide, wide, w_a.resident(), w_b.resident(),
                  w_o.resident(), _resident(g.shape), _resident(b.shape)],
        out_specs=wide,
        out_shape=jax.ShapeDtypeStruct((m, d), F32),
        scratch_shapes=[pltpu.VMEM((tm, d), BF16)],
        compiler_params=_params("arbitrary"),
        name="mix_out",
    )(x, att, cbr, ga, gb, w_a.stack, w_b.stack, w_o.stack, g, b)


def _proj_ln_kernel(x_ref, y_ref, w_ref, g_ref, b_ref, o_ref):
    o_ref[...] = _layer_norm(ALPHA * x_ref[...] + _dot(y_ref[...], w_ref[...]),
                             g_ref[...], b_ref[...])


def _proj_ln(x, y, w, g, b, tm):
    m, d = x.shape
    kw = y.shape[1]
    return pl.pallas_call(
        _proj_ln_kernel,
        grid=(m // tm,),
        in_specs=[pl.BlockSpec((tm, d), lambda i: (i, 0)),
                  pl.BlockSpec((tm, kw), lambda i: (i, 0)),
                  w.resident(), _resident(g.shape), _resident(b.shape)],
        out_specs=pl.BlockSpec((tm, d), lambda i: (i, 0)),
        out_shape=jax.ShapeDtypeStruct((m, d), F32),
        compiler_params=_params("arbitrary"),
        name="proj_ln",
    )(x, y, w.stack, g, b)


def _proj_kernel(x_ref, w_ref, o_ref):
    o_ref[...] = _dot(x_ref[...].astype(BF16), w_ref[...]).astype(o_ref.dtype)


def _proj(x, w, out_dtype, tm):
    m, d = x.shape
    nw = w.shape[1]
    return pl.pallas_call(
        _proj_kernel,
        grid=(m // tm,),
        in_specs=[pl.BlockSpec((tm, d), lambda i: (i, 0)), w.resident()],
        out_specs=pl.BlockSpec((tm, nw), lambda i: (i, 0)),
        out_shape=jax.ShapeDtypeStruct((m, nw), out_dtype),
        compiler_params=_params("arbitrary"),
        name="proj",
    )(x, w.stack)


def _xattn_head(q, kb, vb, cs):
    s = lax.dot_general(q, kb[:, cs], _NT, preferred_element_type=F32) * q.shape[1] ** -0.5
    e = jnp.exp(s - jnp.max(s, axis=-1, keepdims=True))
    p = e / jnp.sum(e, axis=-1, keepdims=True)
    return _dot(p.astype(BF16), vb[:, cs]).astype(BF16)


def _xattn_kernel(q_ref, mk_ref, mv_ref, o_ref, kb, vb):
    @pl.when(pl.program_id(1) == 0)
    def _():
        kb[...] = mk_ref[...].astype(BF16)
        vb[...] = mv_ref[...].astype(BF16)

    hd = q_ref.shape[1] // X_HEADS
    for h in range(X_HEADS):
        cs = slice(h * hd, (h + 1) * hd)
        o_ref[:, cs] = _xattn_head(q_ref[:, cs], kb, vb, cs)


def _xattn_ln_kernel(x_ref, mk_ref, mv_ref, wq_ref, wo_ref, g_ref, b_ref, o_ref, kb, vb, att,
                     *, tiles_per_seq):
    @pl.when(pl.program_id(0) % tiles_per_seq == 0)
    def _():
        kb[...] = mk_ref[...].astype(BF16)
        vb[...] = mv_ref[...].astype(BF16)

    x = x_ref[...]
    xb = x.astype(BF16)
    hd = x.shape[1] // X_HEADS
    heads = [slice(h * hd, (h + 1) * hd) for h in range(X_HEADS)]
    qs = [_dot(xb, wq_ref[:, cs]).astype(BF16) for cs in heads]
    ss = [lax.dot_general(q, kb[:, cs], _NT, preferred_element_type=F32) * hd ** -0.5
          for q, cs in zip(qs, heads)]
    es = [jnp.exp(s - jnp.max(s, axis=-1, keepdims=True)) for s in ss]
    ps = [(e / jnp.sum(e, axis=-1, keepdims=True)).astype(BF16) for e in es]
    for p, cs in zip(ps, heads):
        att[:, cs] = _dot(p, vb[:, cs]).astype(BF16)
    o_ref[...] = _layer_norm(ALPHA * x + _dot(att[...], wo_ref[...]), g_ref[...], b_ref[...])


def _cross_attn_ln(x, mk, mv, w_q, w_o, g, b, t, n_mem, tm):
    m, d = x.shape
    tiles_per_seq = t // tm
    mem = pl.BlockSpec((n_mem, d), lambda i: (i // tiles_per_seq, 0))
    tile = pl.BlockSpec((tm, d), lambda i: (i, 0))
    return pl.pallas_call(
        functools.partial(_xattn_ln_kernel, tiles_per_seq=tiles_per_seq),
        grid=(m // tm,),
        in_specs=[tile, mem, mem, w_q.resident(), w_o.resident(),
                  _resident(g.shape), _resident(b.shape)],
        out_specs=tile,
        out_shape=jax.ShapeDtypeStruct((m, d), F32),
        scratch_shapes=[pltpu.VMEM((n_mem, d), BF16), pltpu.VMEM((n_mem, d), BF16),
                        pltpu.VMEM((tm, d), BF16)],
        compiler_params=_params("arbitrary"),
        name="cross_attn_ln",
    )(x, mk, mv, w_q.stack, w_o.stack, g, b)


def _cross_attn(q, mk, mv, n, t, n_mem, mem_block0, tq):
    m, d = q.shape
    nq = t // tq
    mem = pl.BlockSpec((n_mem, d), lambda b, i: (mem_block0 + b, 0))
    return pl.pallas_call(
        _xattn_kernel,
        grid=(n, nq),
        in_specs=[pl.BlockSpec((tq, d), lambda b, i: (b * nq + i, 0)), mem, mem],
        out_specs=pl.BlockSpec((tq, d), lambda b, i: (b * nq + i, 0)),
        out_shape=jax.ShapeDtypeStruct((m, d), BF16),
        scratch_shapes=[pltpu.VMEM((n_mem, d), BF16), pltpu.VMEM((n_mem, d), BF16)],
        compiler_params=_params("arbitrary", "arbitrary"),
        name="cross_attn",
    )(q, mk, mv)


def _ffn_kernel(*refs, seg, tiles_per_seq, carried, tail_rows, sub):
    if carried:
        (x_ref, wa_ref, wg_ref, wd_ref, cw_ref, cb_ref, g_ref, b_ref,
         o_ref, tail_ref, xb, acc, carry) = refs
    else:
        (x_ref, wa_ref, wg_ref, wd_ref, cw_ref, cb_ref, g_ref, b_ref, s0_ref, s1_ref,
         o_ref, tail_ref, xb, acc) = refs
    i, j = pl.program_id(0), pl.program_id(1)
    tm, tf = x_ref.shape[0], wa_ref.shape[1]

    @pl.when(j == 0)
    def _():
        xb[...] = x_ref[...].astype(BF16)
        acc[...] = jnp.zeros_like(acc)

    if carried:
        @pl.when(i == 0)
        def _():
            carry[j] = jnp.zeros((SUBLANES, tf), F32)

    x = xb[...]
    cols = [slice(k * sub, (k + 1) * sub) for k in range(tf // sub)]
    ups = [(_dot(x, wa_ref[:, cs]), _dot(x, wg_ref[:, cs])) for cs in cols]
    t = lax.broadcasted_iota(jnp.int32, (tm, sub), 0) % seg
    down = None
    for cs, (a, gate) in zip(cols, ups):
        if carried:
            prev = jnp.where(i % tiles_per_seq != 0, carry[j, :, cs], 0.0)
            s0 = prev[SUBLANES - 2:SUBLANES - 1]
            s1 = prev[SUBLANES - 1:SUBLANES]
            carry[j, :, cs] = a[tm - SUBLANES:]
        else:
            s0, s1 = s0_ref[:, cs], s1_ref[:, cs]
        p1 = jnp.where(t == 0, s1, pltpu.roll(a, 1, axis=0))
        p2 = jnp.where(t == 0, s0, jnp.where(t == 1, s1, pltpu.roll(a, 2, axis=0)))
        a_c = cw_ref[0:1, cs] * p2 + cw_ref[1:2, cs] * p1 + cw_ref[2:3, cs] * a + cb_ref[:, cs]
        y = 0.5 * a_c * (1.0 + lax.erf(a_c * math.sqrt(0.5))) * gate
        part = _dot(y.astype(BF16), wd_ref[cs, :])
        down = part if down is None else down + part
        tail_ref[:, cs] = a[tm - tail_rows:]
    acc[...] += down

    @pl.when(j == pl.num_programs(1) - 1)
    def _():
        o_ref[...] = _layer_norm(ALPHA * x_ref[...] + acc[...], g_ref[...], b_ref[...])


def _conv_ffn(x, w_up, cw, cb, w_down, g, b, seg, tm, state_rows=None, tf=512, sub=256):
    m, d = x.shape
    f = w_down.shape[0]
    nj = f // tf
    carried = state_rows is None
    tail_rows = SUBLANES if carried else tm
    tiles_per_seq = max(seg // tm, 1)
    vec_d = pl.BlockSpec((1, d), lambda i, j: (0, 0))
    in_specs = [pl.BlockSpec((tm, d), lambda i, j: (i, 0)),
                w_up.spec((d, tf), lambda i, j: (0, j)),
                w_up.spec((d, tf), lambda i, j: (0, nj + j)),
                w_down.spec((tf, d), lambda i, j: (j, 0)),
                pl.BlockSpec((FFN_CONV_K, tf), lambda i, j: (0, j)),
                pl.BlockSpec((1, tf), lambda i, j: (0, j)),
                vec_d, vec_d]
    args = [x, w_up.stack, w_up.stack, w_down.stack, cw, cb, g, b]
    scratch = [pltpu.VMEM((tm, d), BF16), pltpu.VMEM((tm, d), F32)]
    if carried:
        scratch.append(pltpu.VMEM((nj, SUBLANES, tf), F32))
    else:
        in_specs += [pl.BlockSpec((tm, tf), lambda i, j: (i, j))] * 2
        args += list(state_rows)
    return pl.pallas_call(
        functools.partial(_ffn_kernel, seg=seg, tiles_per_seq=tiles_per_seq, carried=carried,
                          tail_rows=tail_rows, sub=sub),
        grid=(m // tm, nj),
        in_specs=in_specs,
        out_specs=[pl.BlockSpec((tm, d), lambda i, j: (i, 0)),
                   pl.BlockSpec((tail_rows, tf), lambda i, j: (i, j))],
        out_shape=[jax.ShapeDtypeStruct((m, d), F32),
                   jax.ShapeDtypeStruct((m // tm * tail_rows, f), F32)],
        scratch_shapes=scratch,
        compiler_params=_params("arbitrary", "arbitrary"),
        name="conv_ffn",
    )(*args)


def _row(v):
    return v.reshape(1, -1)


def _mixing(x, t, w, hist, attend, tm):
    q, k, v, kb, vb, u, ga, gb = _in_proj(x, w["w_in"], tm)
    att = attend(q, kb, vb)
    lead = CONV_PAD - CONV_STATE
    hist = jnp.pad(hist, ((0, 0), (lead, 0), (0, 0)))
    conv_w = jnp.pad(w["conv_w"], ((lead, CONV_PAD + SUBLANES - lead - CONV_K), (0, 0)))
    cbr = _conv_branch(u, hist, conv_w, w["conv_b"], w["conv_ln_g"], w["conv_ln_b"], t,
                       tt=min(t, 256))
    x = _mix_out(x, att, cbr, ga, gb, w["w_a"], w["w_b"], w["w_o"], w["ln1_g"], w["ln1_b"],
                 min(tm, 256))
    return x, k, v, u


def kernel(x_prompt, x_sample, cache_k, cache_v, cache_mem_k, cache_mem_v, state_conv, state_ffn,
           page_table, mem_prompt, w_in, lam, subln_g, w_a, conv_w, conv_b, conv_ln_g, conv_ln_b,
           w_b, w_o, ln1_g, ln1_b, w_xq, w_xk, w_xv, w_xo, ln2_g, ln2_b, w_up, ffn_conv_w,
           ffn_conv_b, w_down, ln3_g, ln3_b):
    n_p, t_p, d = x_prompt.shape
    n_s, t_s, _ = x_sample.shape
    depth = w_in.shape[0]
    aw = d // 2
    nh = aw // HEAD_W
    n_mem = mem_prompt.shape[1]
    f = w_down.shape[1]
    page = cache_k.shape[2]
    m_p, m_s = n_p * t_p, n_s * t_s
    tm_p, tm_s = min(512, m_p), m_s

    cache_k2 = cache_k.reshape(depth, -1, page * nh, HEAD_W)
    cache_v2 = cache_v.reshape(depth, -1, page * nh, HEAD_W)
    mem_k_s = cache_mem_k.reshape(depth * n_s * n_mem, d)
    mem_v_s = cache_mem_v.reshape(depth * n_s * n_mem, d)
    mem2 = mem_prompt.reshape(n_p * n_mem, d)

    mats = dict(w_in=w_in, w_a=w_a, w_b=w_b, w_o=w_o, w_xq=w_xq, w_xk=w_xk, w_xv=w_xv, w_xo=w_xo,
                w_up=w_up, w_down=w_down)
    mats = {name: stack.astype(BF16) for name, stack in mats.items()}

    xp = x_prompt.reshape(m_p, d)
    xs = x_sample.reshape(m_s, d)
    outs = {name: [] for name in ("kp", "vp", "mkp", "mvp", "cp", "fp", "ks", "vs", "cs", "fs")}
    for l in range(depth):
        lam_init = 0.8 - 0.6 * math.exp(-0.3 * l)
        w = dict({name: _LayerWeight(stack, l) for name, stack in mats.items()},
                 conv_w=conv_w[l], conv_b=_row(conv_b[l]),
                 conv_ln_g=_row(conv_ln_g[l]), conv_ln_b=_row(conv_ln_b[l]),
                 ln1_g=_row(ln1_g[l]), ln1_b=_row(ln1_b[l]), ln2_g=_row(ln2_g[l]),
                 ln2_b=_row(ln2_b[l]), ln3_g=_row(ln3_g[l]), ln3_b=_row(ln3_b[l]),
                 ffn_conv_w=ffn_conv_w[l], ffn_conv_b=_row(ffn_conv_b[l]))

        mk = _proj(mem2, w["w_xk"], F32, min(512, mem2.shape[0]))
        mv = _proj(mem2, w["w_xv"], F32, min(512, mem2.shape[0]))
        attend_p = lambda q, kb, vb: _diff_attn_prompt(
            q, kb, vb, lam[l], subln_g[l].reshape(-1, 1), n_p, t_p, lam_init)
        xp, k, v, u = _mixing(xp, t_p, w, jnp.zeros((n_p, CONV_STATE, aw), F32), attend_p, tm_p)
        xp = _cross_attn_ln(xp, mk, mv, w["w_xq"], w["w_xo"], w["ln2_g"], w["ln2_b"], t_p, n_mem,
                            min(256, t_p))
        xp, tail = _conv_ffn(xp, w["w_up"], w["ffn_conv_w"], w["ffn_conv_b"], w["w_down"],
                             w["ln3_g"], w["ln3_b"], seg=t_p, tm=tm_p)
        outs["kp"].append(k.reshape(n_p, t_p, nh, HEAD_W))
        outs["vp"].append(v.reshape(n_p, t_p, nh, HEAD_W))
        outs["mkp"].append(mk.reshape(n_p, n_mem, X_HEADS, d // X_HEADS))
        outs["mvp"].append(mv.reshape(n_p, n_mem, X_HEADS, d // X_HEADS))
        outs["cp"].append(u.reshape(n_p, t_p, aw)[:, -CONV_STATE:])
        outs["fp"].append(tail.reshape(n_p, t_p // tm_p, SUBLANES, f)[:, -1, -(FFN_CONV_K - 1):])

        def attend_s(q, kb, vb):
            att = _diff_attn_sample(q.reshape(n_s, t_s, aw), kb.reshape(n_s, t_s, aw),
                                    vb.reshape(n_s, t_s, aw), cache_k2, cache_v2, page_table,
                                    lam[l], _row(subln_g[l]), l, lam_init)
            return att.reshape(m_s, aw)

        xs, k, v, u = _mixing(xs, t_s, w, state_conv[l], attend_s, tm_s)
        xq = _proj(xs, w["w_xq"], BF16, tm_s)
        xa = _cross_attn(xq, mem_k_s, mem_v_s, n_s, t_s, n_mem, l * n_s, t_s)
        xs = _proj_ln(xs, xa, w["w_xo"], w["ln2_g"], w["ln2_b"], tm_s)
        st = state_ffn[l]
        s0 = jnp.repeat(st[:, 0], t_s, axis=0)
        s1 = jnp.repeat(st[:, 1], t_s, axis=0)
        xs, a_s = _conv_ffn(xs, w["w_up"], w["ffn_conv_w"], w["ffn_conv_b"], w["w_down"],
                            w["ln3_g"], w["ln3_b"], seg=t_s, tm=tm_s, state_rows=(s0, s1))
        outs["ks"].append(k.reshape(n_s, t_s, nh, HEAD_W))
        outs["vs"].append(v.reshape(n_s, t_s, nh, HEAD_W))
        u_ext = jnp.concatenate([state_conv[l], u.reshape(n_s, t_s, aw)], axis=1)
        outs["cs"].append(u_ext[:, -CONV_STATE:])
        a_ext = jnp.concatenate([st, a_s.reshape(n_s, t_s, f)], axis=1)
        outs["fs"].append(a_ext[:, -(FFN_CONV_K - 1):])

    stk = {name: jnp.stack(vals) for name, vals in outs.items()}
    return (xp.reshape(n_p, t_p, d), xs.reshape(n_s, t_s, d),
            stk["kp"], stk["vp"], stk["mkp"], stk["mvp"], stk["cp"], stk["fp"],
            stk["ks"], stk["vs"], stk["cs"], stk["fs"])
```

```python
import functools
import math
from typing import NamedTuple

import jax
import jax.numpy as jnp
from jax import lax
from jax.experimental import pallas as pl
from jax.experimental.pallas import tpu as pltpu

F32 = jnp.float32
BF16 = jnp.bfloat16

DEPTH = 4
HEAD_DIM = 64
HEAD_W = 2 * HEAD_DIM
CONV_K = 31
CONV_STATE = CONV_K - 1
CONV_PAD = 32
FFN_CONV_K = 3
X_HEADS = 4
ALPHA = (2.0 * DEPTH) ** 0.25
LN_EPS = 1e-5
QK_SCALE = HEAD_DIM ** -0.5 * math.log2(math.e)

V7X_VMEM_LIMIT_BYTES = 56 * 1024 * 1024
SUBLANES = 8
LANES = 128

_NT = (((1,), (1,)), ((), ()))


def _params(*semantics):
    return pltpu.CompilerParams(dimension_semantics=semantics,
                                vmem_limit_bytes=V7X_VMEM_LIMIT_BYTES)


class _LayerWeight(NamedTuple):
    stack: jax.Array
    layer: int

    @property
    def shape(self):
        return self.stack.shape[1:]

    def spec(self, block, index_map, **kwargs):
        return pl.BlockSpec((None,) + tuple(block), lambda *g: (self.layer,) + tuple(index_map(*g)),
                            **kwargs)

    def resident(self):
        return self.spec(self.shape, lambda *g: (0, 0), pipeline_mode=pl.Buffered(1))


def _layer_norm(y, g, b):
    mu = jnp.mean(y, axis=-1, keepdims=True)
    var = jnp.mean(jnp.square(y - mu), axis=-1, keepdims=True)
    return (y - mu) * lax.rsqrt(var + LN_EPS) * g + b


def _dot(a, b):
    return jnp.dot(a, b, preferred_element_type=F32)


def _lambda_full(lam_ref, lam_init):
    lf = lam_ref[...]
    s1 = jnp.sum(lf[0:1] * lf[1:2], axis=-1, keepdims=True)
    s2 = jnp.sum(lf[2:3] * lf[3:4], axis=-1, keepdims=True)
    return jnp.exp(s1) - jnp.exp(s2) + lam_init


def _sub_norm(o, g, lam_init):
    r = o * lax.rsqrt(jnp.mean(jnp.square(o), axis=-1, keepdims=True) + LN_EPS) * g
    return r * (1.0 - lam_init)


def _inproj_kernel(x_ref, wq, wk, wv, wua, wub, wga, wgb,
                   q_o, k_o, v_o, kb_o, vb_o, u_o, ga_o, gb_o, xb):
    @pl.when(pl.program_id(1) == 0)
    def _():
        xb[...] = x_ref[...].astype(BF16)

    x = xb[...]
    q_o[...] = (_dot(x, wq[...]) * QK_SCALE).astype(BF16)
    k = _dot(x, wk[...])
    k_o[...] = k
    kb_o[...] = k.astype(BF16)
    v = _dot(x, wv[...])
    v_o[...] = v
    vb_o[...] = v.astype(BF16)
    u_o[...] = _dot(x, wua[...]) * jax.nn.sigmoid(_dot(x, wub[...]))
    ga_o[...] = jax.nn.sigmoid(_dot(x, wga[...])).astype(BF16)
    gb_o[...] = jax.nn.sigmoid(_dot(x, wgb[...])).astype(BF16)


def _in_proj(x, w_in, tm, tn=256):
    m, d = x.shape
    aw = d // 2
    nj = aw // tn
    tg = 2 * tn
    offs = (0, aw, 2 * aw, 3 * aw, 4 * aw)
    w_specs = [w_in.spec((d, tn), functools.partial(lambda i, j, o: (0, o + j), o=o // tn))
               for o in offs]
    w_specs += [w_in.spec((d, tg), functools.partial(lambda i, j, o: (0, o + j), o=o // tg))
                for o in (5 * aw, 5 * aw + d)]
    narrow = pl.BlockSpec((tm, tn), lambda i, j: (i, j))
    wide = pl.BlockSpec((tm, tg), lambda i, j: (i, j))
    return pl.pallas_call(
        _inproj_kernel,
        grid=(m // tm, nj),
        in_specs=[pl.BlockSpec((tm, d), lambda i, j: (i, 0))] + w_specs,
        out_specs=[narrow] * 6 + [wide] * 2,
        out_shape=[jax.ShapeDtypeStruct((m, aw), BF16),
                   jax.ShapeDtypeStruct((m, aw), F32),
                   jax.ShapeDtypeStruct((m, aw), F32),
                   jax.ShapeDtypeStruct((m, aw), BF16),
                   jax.ShapeDtypeStruct((m, aw), BF16),
                   jax.ShapeDtypeStruct((m, aw), F32),
                   jax.ShapeDtypeStruct((m, d), BF16),
                   jax.ShapeDtypeStruct((m, d), BF16)],
        scratch_shapes=[pltpu.VMEM((tm, d), BF16)],
        compiler_params=_params("arbitrary", "arbitrary"),
        name="in_proj",
    )(x, *([w_in.stack] * 7))


def _dattn_kernel(lam_ref, g_ref, q_ref, k_ref, v_ref, o_ref, vt, acc, *, tq, lam_init):
    qi = pl.program_id(2)
    hps = q_ref.shape[1] // HEAD_W
    heads = [slice(h * HEAD_W, (h + 1) * HEAD_W) for h in range(hps)]

    @pl.when(qi == 0)
    def _():
        for c in range(vt.shape[1]):
            for h, hs in enumerate(heads):
                vt[h, c] = v_ref[c * tq:(c + 1) * tq, hs].astype(F32).T.astype(BF16)

    lane = lax.broadcasted_iota(jnp.int32, (tq, HEAD_W), 1)
    qm = []
    for hs in heads:
        q = q_ref[:, hs]
        qm.append(jnp.where(lane < HEAD_DIM, q, jnp.zeros_like(q)))
        qm.append(jnp.where(lane >= HEAD_DIM, q, jnp.zeros_like(q)))
    acc[...] = jnp.zeros_like(acc)

    def chunk(c, carry, visible):
        kc = k_ref[pl.ds(pl.multiple_of(c * tq, tq), tq), :]
        vtc = [vt[h, c] for h in range(hps)]
        old = [acc[i] for i in range(len(qm))]
        n = len(qm)
        ms, ls = carry[0::2], carry[1::2]
        ss = [lax.dot_general(kc[:, heads[i // 2]], qm[i], _NT, preferred_element_type=F32)
              for i in range(n)]
        if visible is not None:
            ss = [jnp.where(visible, s, -jnp.inf) for s in ss]
        m_new = [jnp.maximum(ms[i], jnp.max(ss[i], axis=0, keepdims=True)) for i in range(n)]
        ps = [jnp.exp2(ss[i] - m_new[i]) for i in range(n)]
        al = [jnp.exp2(ms[i] - m_new[i]) for i in range(n)]
        l_new = [al[i] * ls[i] + jnp.sum(ps[i], axis=0, keepdims=True) for i in range(n)]
        pv = [_dot(vtc[i // 2], ps[i].astype(BF16)) for i in range(n)]
        for i in range(n):
            acc[i] = al[i] * old[i] + pv[i]
        out = []
        for i in range(n):
            out += [m_new[i], l_new[i]]
        return tuple(out)

    neg = jnp.full((1, tq), -jnp.inf, F32)
    zero = jnp.zeros((1, tq), F32)
    carry = lax.fori_loop(0, qi, lambda c, cr: chunk(c, cr, None), (neg, zero) * len(qm))
    diag = (lax.broadcasted_iota(jnp.int32, (tq, tq), 0)
            <= lax.broadcasted_iota(jnp.int32, (tq, tq), 1))
    carry = chunk(qi, carry, diag)
    lam_full = _lambda_full(lam_ref, lam_init)
    for h, hs in enumerate(heads):
        l1, l2 = carry[4 * h + 1], carry[4 * h + 3]
        o = acc[2 * h] / l1 - lam_full * (acc[2 * h + 1] / l2)
        r = o * lax.rsqrt(jnp.mean(jnp.square(o), axis=0, keepdims=True) + LN_EPS) * g_ref[...]
        o_ref[:, hs] = (r * (1.0 - lam_init)).T.astype(BF16)


def _diff_attn_prompt(q, k, v, lam_l, sub_g_col, n, t, lam_init, tq=256, heads_per_step=8):
    m, aw = q.shape
    nq = t // tq
    wb = heads_per_step * HEAD_W
    return pl.pallas_call(
        functools.partial(_dattn_kernel, tq=tq, lam_init=lam_init),
        grid=(n, aw // wb, nq),
        in_specs=[pl.BlockSpec(lam_l.shape, lambda b, h, i: (0, 0)),
                  pl.BlockSpec((HEAD_W, 1), lambda b, h, i: (0, 0)),
                  pl.BlockSpec((tq, wb), lambda b, h, i: (b * nq + i, h)),
                  pl.BlockSpec((t, wb), lambda b, h, i: (b, h)),
                  pl.BlockSpec((t, wb), lambda b, h, i: (b, h))],
        out_specs=pl.BlockSpec((tq, wb), lambda b, h, i: (b * nq + i, h)),
        out_shape=jax.ShapeDtypeStruct((m, aw), BF16),
        scratch_shapes=[pltpu.VMEM((heads_per_step, nq, HEAD_W, tq), BF16),
                        pltpu.VMEM((2 * heads_per_step, HEAD_W, tq), F32)],
        compiler_params=_params("arbitrary", "arbitrary", "arbitrary"),
        name="diff_attn_prompt",
    )(lam_l, sub_g_col, q, k, v)


def _sattn_kernel(pt_ref, lam_ref, g_ref, q_ref, kn_ref, vn_ref, *rest,
                  pages_per_step, lam_init):
    del pt_ref
    k_refs = rest[:pages_per_step]
    v_refs = rest[pages_per_step:2 * pages_per_step]
    o_ref, qbd, knp, vnp, m_s, l_s, acc_s = rest[2 * pages_per_step:]
    p_id = pl.program_id(1)
    t_new, aw = q_ref.shape[1], q_ref.shape[2]
    nh = aw // HEAD_W
    rows = 2 * t_new
    page = knp.shape[0]

    def update(k_head, v_head, mask):
        s = jnp.concatenate(
            [lax.dot_general(qbd[h * rows:(h + 1) * rows, :], k_head(h),
                             _NT, preferred_element_type=F32) for h in range(nh)], axis=0)
        if mask is not None:
            s = jnp.where(mask, s, -jnp.inf)
        m_old = m_s[...]
        m_new = jnp.maximum(m_old, jnp.max(s, axis=-1, keepdims=True))
        p = jnp.exp2(s - m_new)
        a = jnp.exp2(m_old - m_new)
        l_s[...] = a * l_s[...] + jnp.sum(p, axis=-1, keepdims=True)
        p = p.astype(BF16)
        pv = jnp.concatenate(
            [_dot(p[h * rows:(h + 1) * rows, :], v_head(h)) for h in range(nh)], axis=0)
        acc_s[...] = a * acc_s[...] + pv
        m_s[...] = m_new

    @pl.when(p_id == 0)
    def _():
        lane = lax.broadcasted_iota(jnp.int32, (t_new, HEAD_W), 1)
        for h in range(nh):
            qh = q_ref[0, :, h * HEAD_W:(h + 1) * HEAD_W]
            qbd[h * rows:h * rows + t_new, :] = jnp.where(lane < HEAD_DIM, qh, jnp.zeros_like(qh))
            qbd[h * rows + t_new:(h + 1) * rows, :] = jnp.where(lane >= HEAD_DIM, qh, jnp.zeros_like(qh))
        knp[...] = jnp.zeros_like(knp)
        vnp[...] = jnp.zeros_like(vnp)
        knp[0:t_new, :] = kn_ref[0]
        vnp[0:t_new, :] = vn_ref[0]
        m_s[...] = jnp.full_like(m_s, -jnp.inf)
        l_s[...] = jnp.zeros_like(l_s)
        acc_s[...] = jnp.zeros_like(acc_s)
        r = lax.broadcasted_iota(jnp.int32, (nh * rows, page), 0)
        j = lax.broadcasted_iota(jnp.int32, (nh * rows, page), 1)
        update(lambda h: knp[:, h * HEAD_W:(h + 1) * HEAD_W],
               lambda h: vnp[:, h * HEAD_W:(h + 1) * HEAD_W], j <= r % t_new)

    def head_rows(refs, h):
        return jnp.concatenate([r[pl.ds(h, page, stride=nh), :] for r in refs], axis=0).astype(BF16)

    update(functools.partial(head_rows, k_refs), functools.partial(head_rows, v_refs), None)

    @pl.when(p_id == pl.num_programs(1) - 1)
    def _():
        o_all = acc_s[...] / l_s[...]
        lam_full = _lambda_full(lam_ref, lam_init)
        for h in range(nh):
            o1 = o_all[h * rows:h * rows + t_new, :]
            o2 = o_all[h * rows + t_new:(h + 1) * rows, :]
            o_ref[0, :, h * HEAD_W:(h + 1) * HEAD_W] = _sub_norm(
                o1 - lam_full * o2, g_ref[...], lam_init).astype(BF16)


def _diff_attn_sample(q, k_new, v_new, cache_k, cache_v, page_table, lam_l, sub_g, layer, lam_init,
                      pages_per_step=8):
    n, t_new, aw = q.shape
    nh = aw // HEAD_W
    page = cache_k.shape[2] // nh
    n_pages = page_table.shape[1]
    rows_all = nh * 2 * t_new

    def page_spec(i):
        return pl.BlockSpec((None, None, page * nh, HEAD_W),
                            lambda b, p, pt: (layer, pt[b, p * pages_per_step + i], 0, 0))

    tok_spec = pl.BlockSpec((1, t_new, aw), lambda b, p, pt: (b, 0, 0))
    grid_spec = pltpu.PrefetchScalarGridSpec(
        num_scalar_prefetch=1,
        grid=(n, n_pages // pages_per_step),
        in_specs=[pl.BlockSpec(lam_l.shape, lambda b, p, pt: (0, 0)),
                  pl.BlockSpec((1, HEAD_W), lambda b, p, pt: (0, 0)),
                  tok_spec, tok_spec, tok_spec]
                 + [page_spec(i) for i in range(pages_per_step)] * 2,
        out_specs=tok_spec,
        scratch_shapes=[pltpu.VMEM((rows_all, HEAD_W), BF16),
                        pltpu.VMEM((page, aw), BF16),
                        pltpu.VMEM((page, aw), BF16),
                        pltpu.VMEM((rows_all, 1), F32),
                        pltpu.VMEM((rows_all, 1), F32),
                        pltpu.VMEM((rows_all, HEAD_W), F32)],
    )
    return pl.pallas_call(
        functools.partial(_sattn_kernel, pages_per_step=pages_per_step, lam_init=lam_init),
        grid_spec=grid_spec,
        out_shape=jax.ShapeDtypeStruct((n, t_new, aw), BF16),
        compiler_params=_params("arbitrary", "arbitrary"),
        name="diff_attn_sample",
    )(page_table, lam_l, sub_g, q, k_new, v_new,
      *([cache_k] * pages_per_step), *([cache_v] * pages_per_step))


def _convb_kernel(u_ref, h_ref, w_ref, b_ref, g_ref, be_ref, o_ref, ext, conv_s, *, tt, rt):
    @pl.when(pl.program_id(1) == 0)
    def _():
        ext[0:CONV_PAD, :] = h_ref[0]
        ext[CONV_PAD:, :] = u_ref[...]

    base = pl.multiple_of(pl.program_id(1) * tt, SUBLANES)
    ch = o_ref.shape[1]
    for cb in range(ch // LANES):
        cs = slice(cb * LANES, (cb + 1) * LANES)

        def rows(r, _):
            r0 = pl.multiple_of(base + r * rt, SUBLANES)
            win = ext[pl.ds(r0, rt + CONV_PAD), cs]
            y = b_ref[:, cs] + win[CONV_PAD:CONV_PAD + rt] * w_ref[CONV_PAD:CONV_PAD + 1, cs]
            for s in range(SUBLANES):
                z = win[0:rt + SUBLANES] * w_ref[s:s + 1, cs]
                for a in range(1, CONV_PAD // SUBLANES):
                    o = a * SUBLANES + s
                    z = z + win[a * SUBLANES:a * SUBLANES + rt + SUBLANES] * w_ref[o:o + 1, cs]
                y = y + z[s:s + rt]
            conv_s[pl.ds(pl.multiple_of(r * rt, SUBLANES), rt), cs] = y
            return 0

        lax.fori_loop(0, tt // rt, rows, 0)
    y = _layer_norm(conv_s[...], g_ref[...], be_ref[...])
    o_ref[...] = (y * jax.nn.sigmoid(y)).astype(BF16)


def _conv_branch(u, hist, conv_w, conv_b, ln_g, ln_b, t, tt):
    m, ch = u.shape
    n = m // t
    nt = t // tt
    vec = pl.BlockSpec((1, ch), lambda b, i: (0, 0))
    return pl.pallas_call(
        functools.partial(_convb_kernel, tt=tt, rt=min(tt, 64)),
        grid=(n, nt),
        in_specs=[pl.BlockSpec((t, ch), lambda b, i: (b, 0)),
                  pl.BlockSpec((1, CONV_PAD, ch), lambda b, i: (b, 0, 0)),
                  pl.BlockSpec(conv_w.shape, lambda b, i: (0, 0)), vec, vec, vec],
        out_specs=pl.BlockSpec((tt, ch), lambda b, i: (b * nt + i, 0)),
        out_shape=jax.ShapeDtypeStruct((m, ch), BF16),
        scratch_shapes=[pltpu.VMEM((CONV_PAD + t, ch), F32), pltpu.VMEM((tt, ch), F32)],
        compiler_params=_params("arbitrary", "arbitrary"),
        name="conv_branch",
    )(u, hist, conv_w, conv_b, ln_g, ln_b)


def _mix_out_kernel(x_ref, a_ref, c_ref, ga_ref, gb_ref, wa_ref, wb_ref, wo_ref, g_ref, b_ref,
                    o_ref, merged, *, tn):
    att, cbr = a_ref[...], c_ref[...]
    for j in range(o_ref.shape[1] // tn):
        cs = slice(j * tn, (j + 1) * tn)
        merged[:, cs] = (ga_ref[:, cs] * _dot(att, wa_ref[:, cs])
                         + gb_ref[:, cs] * _dot(cbr, wb_ref[:, cs])).astype(BF16)
    o_ref[...] = _layer_norm(ALPHA * x_ref[...] + _dot(merged[...], wo_ref[...]),
                             g_ref[...], b_ref[...])


def _resident(shape):
    return pl.BlockSpec(shape, lambda i: (0,) * len(shape), pipeline_mode=pl.Buffered(1))


def _mix_out(x, att, cbr, ga, gb, w_a, w_b, w_o, g, b, tm, tn=512):
    m, d = x.shape
    kw = att.shape[1]
    wide = pl.BlockSpec((tm, d), lambda i: (i, 0))
    narrow = pl.BlockSpec((tm, kw), lambda i: (i, 0))
    return pl.pallas_call(
        functools.partial(_mix_out_kernel, tn=tn),
        grid=(m // tm,),
        in_specs=[wide, narrow, narrow, wide, wide, w_a.resident(), w_b.resident(),
                  w_o.resident(), _resident(g.shape), _resident(b.shape)],
        out_specs=wide,
        out_shape=jax.ShapeDtypeStruct((m, d), F32),
        scratch_shapes=[pltpu.VMEM((tm, d), BF16)],
        compiler_params=_params("arbitrary"),
        name="mix_out",
    )(x, att, cbr, ga, gb, w_a.stack, w_b.stack, w_o.stack, g, b)


def _proj_ln_kernel(x_ref, y_ref, w_ref, g_ref, b_ref, o_ref):
    o_ref[...] = _layer_norm(ALPHA * x_ref[...] + _dot(y_ref[...], w_ref[...]),
                             g_ref[...], b_ref[...])


def _proj_ln(x, y, w, g, b, tm):
    m, d = x.shape
    kw = y.shape[1]
    return pl.pallas_call(
        _proj_ln_kernel,
        grid=(m // tm,),
        in_specs=[pl.BlockSpec((tm, d), lambda i: (i, 0)),
                  pl.BlockSpec((tm, kw), lambda i: (i, 0)),
                  w.resident(), _resident(g.shape), _resident(b.shape)],
        out_specs=pl.BlockSpec((tm, d), lambda i: (i, 0)),
        out_shape=jax.ShapeDtypeStruct((m, d), F32),
        compiler_params=_params("arbitrary"),
        name="proj_ln",
    )(x, y, w.stack, g, b)


def _proj_kernel(x_ref, w_ref, o_ref):
    o_ref[...] = _dot(x_ref[...].astype(BF16), w_ref[...]).astype(o_ref.dtype)


def _proj(x, w, out_dtype, tm):
    m, d = x.shape
    nw = w.shape[1]
    return pl.pallas_call(
        _proj_kernel,
        grid=(m // tm,),
        in_specs=[pl.BlockSpec((tm, d), lambda i: (i, 0)), w.resident()],
        out_specs=pl.BlockSpec((tm, nw), lambda i: (i, 0)),
        out_shape=jax.ShapeDtypeStruct((m, nw), out_dtype),
        compiler_params=_params("arbitrary"),
        name="proj",
    )(x, w.stack)


def _xattn_head(q, kb, vb, cs):
    s = lax.dot_general(q, kb[:, cs], _NT, preferred_element_type=F32) * q.shape[1] ** -0.5
    e = jnp.exp(s - jnp.max(s, axis=-1, keepdims=True))
    p = e / jnp.sum(e, axis=-1, keepdims=True)
    return _dot(p.astype(BF16), vb[:, cs]).astype(BF16)


def _xattn_kernel(q_ref, mk_ref, mv_ref, o_ref, kb, vb):
    @pl.when(pl.program_id(1) == 0)
    def _():
        kb[...] = mk_ref[...].astype(BF16)
        vb[...] = mv_ref[...].astype(BF16)

    hd = q_ref.shape[1] // X_HEADS
    for h in range(X_HEADS):
        cs = slice(h * hd, (h + 1) * hd)
        o_ref[:, cs] = _xattn_head(q_ref[:, cs], kb, vb, cs)


def _xattn_ln_kernel(x_ref, mk_ref, mv_ref, wq_ref, wo_ref, g_ref, b_ref, o_ref, kb, vb, att,
                     *, tiles_per_seq):
    @pl.when(pl.program_id(0) % tiles_per_seq == 0)
    def _():
        kb[...] = mk_ref[...].astype(BF16)
        vb[...] = mv_ref[...].astype(BF16)

    x = x_ref[...]
    xb = x.astype(BF16)
    hd = x.shape[1] // X_HEADS
    heads = [slice(h * hd, (h + 1) * hd) for h in range(X_HEADS)]
    qs = [_dot(xb, wq_ref[:, cs]).astype(BF16) for cs in heads]
    ss = [lax.dot_general(q, kb[:, cs], _NT, preferred_element_type=F32) * hd ** -0.5
          for q, cs in zip(qs, heads)]
    es = [jnp.exp(s - jnp.max(s, axis=-1, keepdims=True)) for s in ss]
    ps = [(e / jnp.sum(e, axis=-1, keepdims=True)).astype(BF16) for e in es]
    for p, cs in zip(ps, heads):
        att[:, cs] = _dot(p, vb[:, cs]).astype(BF16)
    o_ref[...] = _layer_norm(ALPHA * x + _dot(att[...], wo_ref[...]), g_ref[...], b_ref[...])


def _cross_attn_ln(x, mk, mv, w_q, w_o, g, b, t, n_mem, tm):
    m, d = x.shape
    tiles_per_seq = t // tm
    mem = pl.BlockSpec((n_mem, d), lambda i: (i // tiles_per_seq, 0))
    tile = pl.BlockSpec((tm, d), lambda i: (i, 0))
    return pl.pallas_call(
        functools.partial(_xattn_ln_kernel, tiles_per_seq=tiles_per_seq),
        grid=(m // tm,),
        in_specs=[tile, mem, mem, w_q.resident(), w_o.resident(),
                  _resident(g.shape), _resident(b.shape)],
        out_specs=tile,
        out_shape=jax.ShapeDtypeStruct((m, d), F32),
        scratch_shapes=[pltpu.VMEM((n_mem, d), BF16), pltpu.VMEM((n_mem, d), BF16),
                        pltpu.VMEM((tm, d), BF16)],
        compiler_params=_params("arbitrary"),
        name="cross_attn_ln",
    )(x, mk, mv, w_q.stack, w_o.stack, g, b)


def _cross_attn(q, mk, mv, n, t, n_mem, mem_block0, tq):
    m, d = q.shape
    nq = t // tq
    mem = pl.BlockSpec((n_mem, d), lambda b, i: (mem_block0 + b, 0))
    return pl.pallas_call(
        _xattn_kernel,
        grid=(n, nq),
        in_specs=[pl.BlockSpec((tq, d), lambda b, i: (b * nq + i, 0)), mem, mem],
        out_specs=pl.BlockSpec((tq, d), lambda b, i: (b * nq + i, 0)),
        out_shape=jax.ShapeDtypeStruct((m, d), BF16),
        scratch_shapes=[pltpu.VMEM((n_mem, d), BF16), pltpu.VMEM((n_mem, d), BF16)],
        compiler_params=_params("arbitrary", "arbitrary"),
        name="cross_attn",
    )(q, mk, mv)


def _ffn_kernel(*refs, seg, tiles_per_seq, carried, tail_rows, sub):
    if carried:
        (x_ref, wa_ref, wg_ref, wd_ref, cw_ref, cb_ref, g_ref, b_ref,
         o_ref, tail_ref, xb, acc, carry) = refs
    else:
        (x_ref, wa_ref, wg_ref, wd_ref, cw_ref, cb_ref, g_ref, b_ref, s0_ref, s1_ref,
         o_ref, tail_ref, xb, acc) = refs
    i, j = pl.program_id(0), pl.program_id(1)
    tm, tf = x_ref.shape[0], wa_ref.shape[1]

    @pl.when(j == 0)
    def _():
        xb[...] = x_ref[...].astype(BF16)
        acc[...] = jnp.zeros_like(acc)

    if carried:
        @pl.when(i == 0)
        def _():
            carry[j] = jnp.zeros((SUBLANES, tf), F32)

    x = xb[...]
    cols = [slice(k * sub, (k + 1) * sub) for k in range(tf // sub)]
    ups = [(_dot(x, wa_ref[:, cs]), _dot(x, wg_ref[:, cs])) for cs in cols]
    t = lax.broadcasted_iota(jnp.int32, (tm, sub), 0) % seg
    down = None
    for cs, (a, gate) in zip(cols, ups):
        if carried:
            prev = jnp.where(i % tiles_per_seq != 0, carry[j, :, cs], 0.0)
            s0 = prev[SUBLANES - 2:SUBLANES - 1]
            s1 = prev[SUBLANES - 1:SUBLANES]
            carry[j, :, cs] = a[tm - SUBLANES:]
        else:
            s0, s1 = s0_ref[:, cs], s1_ref[:, cs]
        p1 = jnp.where(t == 0, s1, pltpu.roll(a, 1, axis=0))
        p2 = jnp.where(t == 0, s0, jnp.where(t == 1, s1, pltpu.roll(a, 2, axis=0)))
        a_c = cw_ref[0:1, cs] * p2 + cw_ref[1:2, cs] * p1 + cw_ref[2:3, cs] * a + cb_ref[:, cs]
        y = 0.5 * a_c * (1.0 + lax.erf(a_c * math.sqrt(0.5))) * gate
        part = _dot(y.astype(BF16), wd_ref[cs, :])
        down = part if down is None else down + part
        tail_ref[:, cs] = a[tm - tail_rows:]
    acc[...] += down

    @pl.when(j == pl.num_programs(1) - 1)
    def _():
        o_ref[...] = _layer_norm(ALPHA * x_ref[...] + acc[...], g_ref[...], b_ref[...])


def _conv_ffn(x, w_up, cw, cb, w_down, g, b, seg, tm, state_rows=None, tf=512, sub=256):
    m, d = x.shape
    f = w_down.shape[0]
    nj = f // tf
    carried = state_rows is None
    tail_rows = SUBLANES if carried else tm
    tiles_per_seq = max(seg // tm, 1)
    vec_d = pl.BlockSpec((1, d), lambda i, j: (0, 0))
    in_specs = [pl.BlockSpec((tm, d), lambda i, j: (i, 0)),
                w_up.spec((d, tf), lambda i, j: (0, j)),
                w_up.spec((d, tf), lambda i, j: (0, nj + j)),
                w_down.spec((tf, d), lambda i, j: (j, 0)),
                pl.BlockSpec((FFN_CONV_K, tf), lambda i, j: (0, j)),
                pl.BlockSpec((1, tf), lambda i, j: (0, j)),
                vec_d, vec_d]
    args = [x, w_up.stack, w_up.stack, w_down.stack, cw, cb, g, b]
    scratch = [pltpu.VMEM((tm, d), BF16), pltpu.VMEM((tm, d), F32)]
    if carried:
        scratch.append(pltpu.VMEM((nj, SUBLANES, tf), F32))
    else:
        in_specs += [pl.BlockSpec((tm, tf), lambda i, j: (i, j))] * 2
        args += list(state_rows)
    return pl.pallas_call(
        functools.partial(_ffn_kernel, seg=seg, tiles_per_seq=tiles_per_seq, carried=carried,
                          tail_rows=tail_rows, sub=sub),
        grid=(m // tm, nj),
        in_specs=in_specs,
        out_specs=[pl.BlockSpec((tm, d), lambda i, j: (i, 0)),
                   pl.BlockSpec((tail_rows, tf), lambda i, j: (i, j))],
        out_shape=[jax.ShapeDtypeStruct((m, d), F32),
                   jax.ShapeDtypeStruct((m // tm * tail_rows, f), F32)],
        scratch_shapes=scratch,
        compiler_params=_params("arbitrary", "arbitrary"),
        name="conv_ffn",
    )(*args)


def _row(v):
    return v.reshape(1, -1)


def _mixing(x, t, w, hist, attend, tm):
    q, k, v, kb, vb, u, ga, gb = _in_proj(x, w["w_in"], tm)
    att = attend(q, kb, vb)
    lead = CONV_PAD - CONV_STATE
    hist = jnp.pad(hist, ((0, 0), (lead, 0), (0, 0)))
    conv_w = jnp.pad(w["conv_w"], ((lead, CONV_PAD + SUBLANES - lead - CONV_K), (0, 0)))
    cbr = _conv_branch(u, hist, conv_w, w["conv_b"], w["conv_ln_g"], w["conv_ln_b"], t,
                       tt=min(t, 256))
    x = _mix_out(x, att, cbr, ga, gb, w["w_a"], w["w_b"], w["w_o"], w["ln1_g"], w["ln1_b"],
                 min(tm, 256))
    return x, k, v, u


def kernel(x_prompt, x_sample, cache_k, cache_v, cache_mem_k, cache_mem_v, state_conv, state_ffn,
           page_table, mem_prompt, w_in, lam, subln_g, w_a, conv_w, conv_b, conv_ln_g, conv_ln_b,
           w_b, w_o, ln1_g, ln1_b, w_xq, w_xk, w_xv, w_xo, ln2_g, ln2_b, w_up, ffn_conv_w,
           ffn_conv_b, w_down, ln3_g, ln3_b):
    n_p, t_p, d = x_prompt.shape
    n_s, t_s, _ = x_sample.shape
    depth = w_in.shape[0]
    aw = d // 2
    nh = aw // HEAD_W
    n_mem = mem_prompt.shape[1]
    f = w_down.shape[1]
    page = cache_k.shape[2]
    m_p, m_s = n_p * t_p, n_s * t_s
    tm_p, tm_s = min(512, m_p), m_s

    cache_k2 = cache_k.reshape(depth, -1, page * nh, HEAD_W)
    cache_v2 = cache_v.reshape(depth, -1, page * nh, HEAD_W)
    mem_k_s = cache_mem_k.reshape(depth * n_s * n_mem, d)
    mem_v_s = cache_mem_v.reshape(depth * n_s * n_mem, d)
    mem2 = mem_prompt.reshape(n_p * n_mem, d)

    mats = dict(w_in=w_in, w_a=w_a, w_b=w_b, w_o=w_o, w_xq=w_xq, w_xk=w_xk, w_xv=w_xv, w_xo=w_xo,
                w_up=w_up, w_down=w_down)
    mats = {name: stack.astype(BF16) for name, stack in mats.items()}

    xp = x_prompt.reshape(m_p, d)
    xs = x_sample.reshape(m_s, d)
    outs = {name: [] for name in ("kp", "vp", "mkp", "mvp", "cp", "fp", "ks", "vs", "cs", "fs")}
    for l in range(depth):
        lam_init = 0.8 - 0.6 * math.exp(-0.3 * l)
        w = dict({name: _LayerWeight(stack, l) for name, stack in mats.items()},
                 conv_w=conv_w[l], conv_b=_row(conv_b[l]),
                 conv_ln_g=_row(conv_ln_g[l]), conv_ln_b=_row(conv_ln_b[l]),
                 ln1_g=_row(ln1_g[l]), ln1_b=_row(ln1_b[l]), ln2_g=_row(ln2_g[l]),
                 ln2_b=_row(ln2_b[l]), ln3_g=_row(ln3_g[l]), ln3_b=_row(ln3_b[l]),
                 ffn_conv_w=ffn_conv_w[l], ffn_conv_b=_row(ffn_conv_b[l]))

        mk = _proj(mem2, w["w_xk"], F32, min(512, mem2.shape[0]))
        mv = _proj(mem2, w["w_xv"], F32, min(512, mem2.shape[0]))
        attend_p = lambda q, kb, vb: _diff_attn_prompt(
            q, kb, vb, lam[l], subln_g[l].reshape(-1, 1), n_p, t_p, lam_init)
        xp, k, v, u = _mixing(xp, t_p, w, jnp.zeros((n_p, CONV_STATE, aw), F32), attend_p, tm_p)
        xp = _cross_attn_ln(xp, mk, mv, w["w_xq"], w["w_xo"], w["ln2_g"], w["ln2_b"], t_p, n_mem,
                            min(256, t_p))
        xp, tail = _conv_ffn(xp, w["w_up"], w["ffn_conv_w"], w["ffn_conv_b"], w["w_down"],
                             w["ln3_g"], w["ln3_b"], seg=t_p, tm=tm_p)
        outs["kp"].append(k.reshape(n_p, t_p, nh, HEAD_W))
        outs["vp"].append(v.reshape(n_p, t_p, nh, HEAD_W))
        outs["mkp"].append(mk.reshape(n_p, n_mem, X_HEADS, d // X_HEADS))
        outs["mvp"].append(mv.reshape(n_p, n_mem, X_HEADS, d // X_HEADS))
        outs["cp"].append(u.reshape(n_p, t_p, aw)[:, -CONV_STATE:])
        outs["fp"].append(tail.reshape(n_p, t_p // tm_p, SUBLANES, f)[:, -1, -(FFN_CONV_K - 1):])

        def attend_s(q, kb, vb):
            att = _diff_attn_sample(q.reshape(n_s, t_s, aw), kb.reshape(n_s, t_s, aw),
                                    vb.reshape(n_s, t_s, aw), cache_k2, cache_v2, page_table,
                                    lam[l], _row(subln_g[l]), l, lam_init)
            return att.reshape(m_s, aw)

        xs, k, v, u = _mixing(xs, t_s, w, state_conv[l], attend_s, tm_s)
        xq = _proj(xs, w["w_xq"], BF16, tm_s)
        xa = _cross_attn(xq, mem_k_s, mem_v_s, n_s, t_s, n_mem, l * n_s, t_s)
        xs = _proj_ln(xs, xa, w["w_xo"], w["ln2_g"], w["ln2_b"], tm_s)
        st = state_ffn[l]
        s0 = jnp.repeat(st[:, 0], t_s, axis=0)
        s1 = jnp.repeat(st[:, 1], t_s, axis=0)
        xs, a_s = _conv_ffn(xs, w["w_up"], w["ffn_conv_w"], w["ffn_conv_b"], w["w_down"],
                            w["ln3_g"], w["ln3_b"], seg=t_s, tm=tm_s, state_rows=(s0, s1))
        outs["ks"].append(k.reshape(n_s, t_s, nh, HEAD_W))
        outs["vs"].append(v.reshape(n_s, t_s, nh, HEAD_W))
        u_ext = jnp.concatenate([state_conv[l], u.reshape(n_s, t_s, aw)], axis=1)
        outs["cs"].append(u_ext[:, -CONV_STATE:])
        a_ext = jnp.concatenate([st, a_s.reshape(n_s, t_s, f)], axis=1)
        outs["fs"].append(a_ext[:, -(FFN_CONV_K - 1):])

    stk = {name: jnp.stack(vals) for name, vals in outs.items()}
    return (xp.reshape(n_p, t_p, d), xs.reshape(n_s, t_s, d),
            stk["kp"], stk["vp"], stk["mkp"], stk["mvp"], stk["cp"], stk["fp"],
            stk["ks"], stk["vs"], stk["cs"], stk["fs"])
```

```python
import functools
import math
from typing import NamedTuple

import jax
import jax.numpy as jnp
from jax import lax
from jax.experimental import pallas as pl
from jax.experimental.pallas import tpu as pltpu

F32 = jnp.float32
BF16 = jnp.bfloat16

DEPTH = 4
HEAD_DIM = 64
HEAD_W = 2 * HEAD_DIM
CONV_K = 31
CONV_STATE = CONV_K - 1
CONV_PAD = 32
FFN_CONV_K = 3
X_HEADS = 4
ALPHA = (2.0 * DEPTH) ** 0.25
LN_EPS = 1e-5
QK_SCALE = HEAD_DIM ** -0.5 * math.log2(math.e)

V7X_VMEM_LIMIT_BYTES = 56 * 1024 * 1024
SUBLANES = 8
LANES = 128

_NT = (((1,), (1,)), ((), ()))


def _params(*semantics):
    return pltpu.CompilerParams(dimension_semantics=semantics,
                                vmem_limit_bytes=V7X_VMEM_LIMIT_BYTES)


class _LayerWeight(NamedTuple):
    stack: jax.Array
    layer: int

    @property
    def shape(self):
        return self.stack.shape[1:]

    def spec(self, block, index_map, **kwargs):
        return pl.BlockSpec((None,) + tuple(block), lambda *g: (self.layer,) + tuple(index_map(*g)),
                            **kwargs)

    def resident(self):
        return self.spec(self.shape, lambda *g: (0, 0), pipeline_mode=pl.Buffered(1))


def _layer_norm(y, g, b):
    mu = jnp.mean(y, axis=-1, keepdims=True)
    var = jnp.mean(jnp.square(y - mu), axis=-1, keepdims=True)
    return (y - mu) * lax.rsqrt(var + LN_EPS) * g + b


def _dot(a, b):
    return jnp.dot(a, b, preferred_element_type=F32)


def _lambda_full(lam_ref, lam_init):
    lf = lam_ref[...]
    s1 = jnp.sum(lf[0:1] * lf[1:2], axis=-1, keepdims=True)
    s2 = jnp.sum(lf[2:3] * lf[3:4], axis=-1, keepdims=True)
    return jnp.exp(s1) - jnp.exp(s2) + lam_init


def _sub_norm(o, g, lam_init):
    r = o * lax.rsqrt(jnp.mean(jnp.square(o), axis=-1, keepdims=True) + LN_EPS) * g
    return r * (1.0 - lam_init)


def _inproj_kernel(x_ref, wq, wk, wv, wua, wub, wga, wgb,
                   q_o, k_o, v_o, kb_o, vb_o, u_o, ga_o, gb_o, xb):
    @pl.when(pl.program_id(1) == 0)
    def _():
        xb[...] = x_ref[...].astype(BF16)

    x = xb[...]
    q_o[...] = (_dot(x, wq[...]) * QK_SCALE).astype(BF16)
    k = _dot(x, wk[...])
    k_o[...] = k
    kb_o[...] = k.astype(BF16)
    v = _dot(x, wv[...])
    v_o[...] = v
    vb_o[...] = v.astype(BF16)
    u_o[...] = _dot(x, wua[...]) * jax.nn.sigmoid(_dot(x, wub[...]))
    ga_o[...] = jax.nn.sigmoid(_dot(x, wga[...])).astype(BF16)
    gb_o[...] = jax.nn.sigmoid(_dot(x, wgb[...])).astype(BF16)


def _in_proj(x, w_in, tm, tn=256):
    m, d = x.shape
    aw = d // 2
    nj = aw // tn
    tg = 2 * tn
    offs = (0, aw, 2 * aw, 3 * aw, 4 * aw)
    w_specs = [w_in.spec((d, tn), functools.partial(lambda i, j, o: (0, o + j), o=o // tn))
               for o in offs]
    w_specs += [w_in.spec((d, tg), functools.partial(lambda i, j, o: (0, o + j), o=o // tg))
                for o in (5 * aw, 5 * aw + d)]
    narrow = pl.BlockSpec((tm, tn), lambda i, j: (i, j))
    wide = pl.BlockSpec((tm, tg), lambda i, j: (i, j))
    return pl.pallas_call(
        _inproj_kernel,
        grid=(m // tm, nj),
        in_specs=[pl.BlockSpec((tm, d), lambda i, j: (i, 0))] + w_specs,
        out_specs=[narrow] * 6 + [wide] * 2,
        out_shape=[jax.ShapeDtypeStruct((m, aw), BF16),
                   jax.ShapeDtypeStruct((m, aw), F32),
                   jax.ShapeDtypeStruct((m, aw), F32),
                   jax.ShapeDtypeStruct((m, aw), BF16),
                   jax.ShapeDtypeStruct((m, aw), BF16),
                   jax.ShapeDtypeStruct((m, aw), F32),
                   jax.ShapeDtypeStruct((m, d), BF16),
                   jax.ShapeDtypeStruct((m, d), BF16)],
        scratch_shapes=[pltpu.VMEM((tm, d), BF16)],
        compiler_params=_params("arbitrary", "arbitrary"),
        name="in_proj",
    )(x, *([w_in.stack] * 7))


def _dattn_kernel(lam_ref, g_ref, q_ref, k_ref, v_ref, o_ref, vt, acc, *, tq, lam_init):
    qi = pl.program_id(2)
    hps = q_ref.shape[1] // HEAD_W
    heads = [slice(h * HEAD_W, (h + 1) * HEAD_W) for h in range(hps)]

    @pl.when(qi == 0)
    def _():
        for c in range(vt.shape[1]):
            for h, hs in enumerate(heads):
                vt[h, c] = v_ref[c * tq:(c + 1) * tq, hs].astype(F32).T.astype(BF16)

    lane = lax.broadcasted_iota(jnp.int32, (tq, HEAD_W), 1)
    qm = []
    for hs in heads:
        q = q_ref[:, hs]
        qm.append(jnp.where(lane < HEAD_DIM, q, jnp.zeros_like(q)))
        qm.append(jnp.where(lane >= HEAD_DIM, q, jnp.zeros_like(q)))
    acc[...] = jnp.zeros_like(acc)

    def chunk(c, carry, visible):
        kc = k_ref[pl.ds(pl.multiple_of(c * tq, tq), tq), :]
        vtc = [vt[h, c] for h in range(hps)]
        old = [acc[i] for i in range(len(qm))]
        n = len(qm)
        ms, ls = carry[0::2], carry[1::2]
        ss = [lax.dot_general(kc[:, heads[i // 2]], qm[i], _NT, preferred_element_type=F32)
              for i in range(n)]
        if visible is not None:
            ss = [jnp.where(visible, s, -jnp.inf) for s in ss]
        m_new = [jnp.maximum(ms[i], jnp.max(ss[i], axis=0, keepdims=True)) for i in range(n)]
        ps = [jnp.exp2(ss[i] - m_new[i]) for i in range(n)]
        al = [jnp.exp2(ms[i] - m_new[i]) for i in range(n)]
        l_new = [al[i] * ls[i] + jnp.sum(ps[i], axis=0, keepdims=True) for i in range(n)]
        pv = [_dot(vtc[i // 2], ps[i].astype(BF16)) for i in range(n)]
        for i in range(n):
            acc[i] = al[i] * old[i] + pv[i]
        out = []
        for i in range(n):
            out += [m_new[i], l_new[i]]
        return tuple(out)

    neg = jnp.full((1, tq), -jnp.inf, F32)
    zero = jnp.zeros((1, tq), F32)
    carry = lax.fori_loop(0, qi, lambda c, cr: chunk(c, cr, None), (neg, zero) * len(qm))
    diag = (lax.broadcasted_iota(jnp.int32, (tq, tq), 0)
            <= lax.broadcasted_iota(jnp.int32, (tq, tq), 1))
    carry = chunk(qi, carry, diag)
    lam_full = _lambda_full(lam_ref, lam_init)
    for h, hs in enumerate(heads):
        l1, l2 = carry[4 * h + 1], carry[4 * h + 3]
        o = acc[2 * h] / l1 - lam_full * (acc[2 * h + 1] / l2)
        r = o * lax.rsqrt(jnp.mean(jnp.square(o), axis=0, keepdims=True) + LN_EPS) * g_ref[...]
        o_ref[:, hs] = (r * (1.0 - lam_init)).T.astype(BF16)


def _diff_attn_prompt(q, k, v, lam_l, sub_g_col, n, t, lam_init, tq=256, heads_per_step=8):
    m, aw = q.shape
    nq = t // tq
    wb = heads_per_step * HEAD_W
    return pl.pallas_call(
        functools.partial(_dattn_kernel, tq=tq, lam_init=lam_init),
        grid=(n, aw // wb, nq),
        in_specs=[pl.BlockSpec(lam_l.shape, lambda b, h, i: (0, 0)),
                  pl.BlockSpec((HEAD_W, 1), lambda b, h, i: (0, 0)),
                  pl.BlockSpec((tq, wb), lambda b, h, i: (b * nq + i, h)),
                  pl.BlockSpec((t, wb), lambda b, h, i: (b, h)),
                  pl.BlockSpec((t, wb), lambda b, h, i: (b, h))],
        out_specs=pl.BlockSpec((tq, wb), lambda b, h, i: (b * nq + i, h)),
        out_shape=jax.ShapeDtypeStruct((m, aw), BF16),
        scratch_shapes=[pltpu.VMEM((heads_per_step, nq, HEAD_W, tq), BF16),
                        pltpu.VMEM((2 * heads_per_step, HEAD_W, tq), F32)],
        compiler_params=_params("arbitrary", "arbitrary", "arbitrary"),
        name="diff_attn_prompt",
    )(lam_l, sub_g_col, q, k, v)


def _sattn_kernel(pt_ref, lam_ref, g_ref, q_ref, kn_ref, vn_ref, *rest,
                  pages_per_step, lam_init):
    del pt_ref
    k_refs = rest[:pages_per_step]
    v_refs = rest[pages_per_step:2 * pages_per_step]
    o_ref, qbd, knp, vnp, m_s, l_s, acc_s = rest[2 * pages_per_step:]
    p_id = pl.program_id(1)
    t_new, aw = q_ref.shape[1], q_ref.shape[2]
    nh = aw // HEAD_W
    rows = 2 * t_new
    page = knp.shape[0]

    def update(k_head, v_head, mask):
        s = jnp.concatenate(
            [lax.dot_general(qbd[h * rows:(h + 1) * rows, :], k_head(h),
                             _NT, preferred_element_type=F32) for h in range(nh)], axis=0)
        if mask is not None:
            s = jnp.where(mask, s, -jnp.inf)
        m_old = m_s[...]
        m_new = jnp.maximum(m_old, jnp.max(s, axis=-1, keepdims=True))
        p = jnp.exp2(s - m_new)
        a = jnp.exp2(m_old - m_new)
        l_s[...] = a * l_s[...] + jnp.sum(p, axis=-1, keepdims=True)
        p = p.astype(BF16)
        pv = jnp.concatenate(
            [_dot(p[h * rows:(h + 1) * rows, :], v_head(h)) for h in range(nh)], axis=0)
        acc_s[...] = a * acc_s[...] + pv
        m_s[...] = m_new

    @pl.when(p_id == 0)
    def _():
        lane = lax.broadcasted_iota(jnp.int32, (t_new, HEAD_W), 1)
        for h in range(nh):
            qh = q_ref[0, :, h * HEAD_W:(h + 1) * HEAD_W]
            qbd[h * rows:h * rows + t_new, :] = jnp.where(lane < HEAD_DIM, qh, jnp.zeros_like(qh))
            qbd[h * rows + t_new:(h + 1) * rows, :] = jnp.where(lane >= HEAD_DIM, qh, jnp.zeros_like(qh))
        knp[...] = jnp.zeros_like(knp)
        vnp[...] = jnp.zeros_like(vnp)
        knp[0:t_new, :] = kn_ref[0]
        vnp[0:t_new, :] = vn_ref[0]
        m_s[...] = jnp.full_like(m_s, -jnp.inf)
        l_s[...] = jnp.zeros_like(l_s)
        acc_s[...] = jnp.zeros_like(acc_s)
        r = lax.broadcasted_iota(jnp.int32, (nh * rows, page), 0)
        j = lax.broadcasted_iota(jnp.int32, (nh * rows, page), 1)
        update(lambda h: knp[:, h * HEAD_W:(h + 1) * HEAD_W],
               lambda h: vnp[:, h * HEAD_W:(h + 1) * HEAD_W], j <= r % t_new)

    def head_rows(refs, h):
        return jnp.concatenate([r[pl.ds(h, page, stride=nh), :] for r in refs], axis=0).astype(BF16)

    update(functools.partial(head_rows, k_refs), functools.partial(head_rows, v_refs), None)

    @pl.when(p_id == pl.num_programs(1) - 1)
    def _():
        o_all = acc_s[...] / l_s[...]
        lam_full = _lambda_full(lam_ref, lam_init)
        for h in range(nh):
            o1 = o_all[h * rows:h * rows + t_new, :]
            o2 = o_all[h * rows + t_new:(h + 1) * rows, :]
            o_ref[0, :, h * HEAD_W:(h + 1) * HEAD_W] = _sub_norm(
                o1 - lam_full * o2, g_ref[...], lam_init).astype(BF16)


def _diff_attn_sample(q, k_new, v_new, cache_k, cache_v, page_table, lam_l, sub_g, layer, lam_init,
                      pages_per_step=16):
    n, t_new, aw = q.shape
    nh = aw // HEAD_W
    page = cache_k.shape[2] // nh
    n_pages = page_table.shape[1]
    assert n_pages % pages_per_step == 0, (n_pages, pages_per_step)
    rows_all = nh * 2 * t_new

    def page_spec(i):
        return pl.BlockSpec((None, None, page * nh, HEAD_W),
                            lambda b, p, pt: (layer, pt[b, p * pages_per_step + i], 0, 0))

    tok_spec = pl.BlockSpec((1, t_new, aw), lambda b, p, pt: (b, 0, 0))
    grid_spec = pltpu.PrefetchScalarGridSpec(
        num_scalar_prefetch=1,
        grid=(n, n_pages // pages_per_step),
        in_specs=[pl.BlockSpec(lam_l.shape, lambda b, p, pt: (0, 0)),
                  pl.BlockSpec((1, HEAD_W), lambda b, p, pt: (0, 0)),
                  tok_spec, tok_spec, tok_spec]
                 + [page_spec(i) for i in range(pages_per_step)] * 2,
        out_specs=tok_spec,
        scratch_shapes=[pltpu.VMEM((rows_all, HEAD_W), BF16),
                        pltpu.VMEM((page, aw), BF16),
                        pltpu.VMEM((page, aw), BF16),
                        pltpu.VMEM((rows_all, 1), F32),
                        pltpu.VMEM((rows_all, 1), F32),
                        pltpu.VMEM((rows_all, HEAD_W), F32)],
    )
    return pl.pallas_call(
        functools.partial(_sattn_kernel, pages_per_step=pages_per_step, lam_init=lam_init),
        grid_spec=grid_spec,
        out_shape=jax.ShapeDtypeStruct((n, t_new, aw), BF16),
        compiler_params=_params("arbitrary", "arbitrary"),
        name="diff_attn_sample",
    )(page_table, lam_l, sub_g, q, k_new, v_new,
      *([cache_k] * pages_per_step), *([cache_v] * pages_per_step))


def _convb_kernel(u_ref, h_ref, w_ref, b_ref, g_ref, be_ref, o_ref, ext, conv_s, *, tt, rt):
    @pl.when(pl.program_id(1) == 0)
    def _():
        ext[0:CONV_PAD, :] = h_ref[0]
        ext[CONV_PAD:, :] = u_ref[...]

    base = pl.multiple_of(pl.program_id(1) * tt, SUBLANES)
    ch = o_ref.shape[1]
    for cb in range(ch // LANES):
        cs = slice(cb * LANES, (cb + 1) * LANES)

        def rows(r, _):
            r0 = pl.multiple_of(base + r * rt, SUBLANES)
            win = ext[pl.ds(r0, rt + CONV_PAD), cs]
            y = b_ref[:, cs] + win[CONV_PAD:CONV_PAD + rt] * w_ref[CONV_PAD:CONV_PAD + 1, cs]
            for s in range(SUBLANES):
                z = win[0:rt + SUBLANES] * w_ref[s:s + 1, cs]
                for a in range(1, CONV_PAD // SUBLANES):
                    o = a * SUBLANES + s
                    z = z + win[a * SUBLANES:a * SUBLANES + rt + SUBLANES] * w_ref[o:o + 1, cs]
                y = y + z[s:s + rt]
            conv_s[pl.ds(pl.multiple_of(r * rt, SUBLANES), rt), cs] = y
            return 0

        lax.fori_loop(0, tt // rt, rows, 0)
    y = _layer_norm(conv_s[...], g_ref[...], be_ref[...])
    o_ref[...] = (y * jax.nn.sigmoid(y)).astype(BF16)


def _conv_branch(u, hist, conv_w, conv_b, ln_g, ln_b, t, tt):
    m, ch = u.shape
    n = m // t
    nt = t // tt
    vec = pl.BlockSpec((1, ch), lambda b, i: (0, 0))
    return pl.pallas_call(
        functools.partial(_convb_kernel, tt=tt, rt=min(tt, 64)),
        grid=(n, nt),
        in_specs=[pl.BlockSpec((t, ch), lambda b, i: (b, 0)),
                  pl.BlockSpec((1, CONV_PAD, ch), lambda b, i: (b, 0, 0)),
                  pl.BlockSpec(conv_w.shape, lambda b, i: (0, 0)), vec, vec, vec],
        out_specs=pl.BlockSpec((tt, ch), lambda b, i: (b * nt + i, 0)),
        out_shape=jax.ShapeDtypeStruct((m, ch), BF16),
        scratch_shapes=[pltpu.VMEM((CONV_PAD + t, ch), F32), pltpu.VMEM((tt, ch), F32)],
        compiler_params=_params("arbitrary", "arbitrary"),
        name="conv_branch",
    )(u, hist, conv_w, conv_b, ln_g, ln_b)


def _mix_out_kernel(x_ref, a_ref, c_ref, ga_ref, gb_ref, wa_ref, wb_ref, wo_ref, g_ref, b_ref,
                    o_ref, merged, *, tn):
    att, cbr = a_ref[...], c_ref[...]
    for j in range(o_ref.shape[1] // tn):
        cs = slice(j * tn, (j + 1) * tn)
        merged[:, cs] = (ga_ref[:, cs] * _dot(att, wa_ref[:, cs])
                         + gb_ref[:, cs] * _dot(cbr, wb_ref[:, cs])).astype(BF16)
    o_ref[...] = _layer_norm(ALPHA * x_ref[...] + _dot(merged[...], wo_ref[...]),
                             g_ref[...], b_ref[...])


def _resident(shape):
    return pl.BlockSpec(shape, lambda i: (0,) * len(shape), pipeline_mode=pl.Buffered(1))


def _mix_out(x, att, cbr, ga, gb, w_a, w_b, w_o, g, b, tm, tn=512):
    m, d = x.shape
    kw = att.shape[1]
    wide = pl.BlockSpec((tm, d), lambda i: (i, 0))
    narrow = pl.BlockSpec((tm, kw), lambda i: (i, 0))
    return pl.pallas_call(
        functools.partial(_mix_out_kernel, tn=tn),
        grid=(m // tm,),
        in_specs=[wide, narrow, narrow, wide, wide, w_a.resident(), w_b.resident(),
                  w_o.resident(), _resident(g.shape), _resident(b.shape)],
        out_specs=wide,
        out_shape=jax.ShapeDtypeStruct((m, d), F32),
        scratch_shapes=[pltpu.VMEM((tm, d), BF16)],
        compiler_params=_params("arbitrary"),
        name="mix_out",
    )(x, att, cbr, ga, gb, w_a.stack, w_b.stack, w_o.stack, g, b)


def _proj_ln_kernel(x_ref, y_ref, w_ref, g_ref, b_ref, o_ref):
    o_ref[...] = _layer_norm(ALPHA * x_ref[...] + _dot(y_ref[...], w_ref[...]),
                             g_ref[...], b_ref[...])


def _proj_ln(x, y, w, g, b, tm):
    m, d = x.shape
    kw = y.shape[1]
    return pl.pallas_call(
        _proj_ln_kernel,
        grid=(m // tm,),
        in_specs=[pl.BlockSpec((tm, d), lambda i: (i, 0)),
                  pl.BlockSpec((tm, kw), lambda i: (i, 0)),
                  w.resident(), _resident(g.shape), _resident(b.shape)],
        out_specs=pl.BlockSpec((tm, d), lambda i: (i, 0)),
        out_shape=jax.ShapeDtypeStruct((m, d), F32),
        compiler_params=_params("arbitrary"),
        name="proj_ln",
    )(x, y, w.stack, g, b)


def _proj_kernel(x_ref, w_ref, o_ref):
    o_ref[...] = _dot(x_ref[...].astype(BF16), w_ref[...]).astype(o_ref.dtype)


def _proj(x, w, out_dtype, tm):
    m, d = x.shape
    nw = w.shape[1]
    return pl.pallas_call(
        _proj_kernel,
        grid=(m // tm,),
        in_specs=[pl.BlockSpec((tm, d), lambda i: (i, 0)), w.resident()],
        out_specs=pl.BlockSpec((tm, nw), lambda i: (i, 0)),
        out_shape=jax.ShapeDtypeStruct((m, nw), out_dtype),
        compiler_params=_params("arbitrary"),
        name="proj",
    )(x, w.stack)


def _xattn_head(q, kb, vb, cs):
    s = lax.dot_general(q, kb[:, cs], _NT, preferred_element_type=F32) * q.shape[1] ** -0.5
    e = jnp.exp(s - jnp.max(s, axis=-1, keepdims=True))
    p = e / jnp.sum(e, axis=-1, keepdims=True)
    return _dot(p.astype(BF16), vb[:, cs]).astype(BF16)


def _xattn_kernel(q_ref, mk_ref, mv_ref, o_ref, kb, vb):
    @pl.when(pl.program_id(1) == 0)
    def _():
        kb[...] = mk_ref[...].astype(BF16)
        vb[...] = mv_ref[...].astype(BF16)

    hd = q_ref.shape[1] // X_HEADS
    for h in range(X_HEADS):
        cs = slice(h * hd, (h + 1) * hd)
        o_ref[:, cs] = _xattn_head(q_ref[:, cs], kb, vb, cs)


def _xattn_ln_kernel(x_ref, mk_ref, mv_ref, wq_ref, wo_ref, g_ref, b_ref, o_ref, kb, vb, att,
                     *, tiles_per_seq):
    @pl.when(pl.program_id(0) % tiles_per_seq == 0)
    def _():
        kb[...] = mk_ref[...].astype(BF16)
        vb[...] = mv_ref[...].astype(BF16)

    x = x_ref[...]
    xb = x.astype(BF16)
    hd = x.shape[1] // X_HEADS
    heads = [slice(h * hd, (h + 1) * hd) for h in range(X_HEADS)]
    qs = [_dot(xb, wq_ref[:, cs]).astype(BF16) for cs in heads]
    ss = [lax.dot_general(q, kb[:, cs], _NT, preferred_element_type=F32) * hd ** -0.5
          for q, cs in zip(qs, heads)]
    es = [jnp.exp(s - jnp.max(s, axis=-1, keepdims=True)) for s in ss]
    ps = [(e / jnp.sum(e, axis=-1, keepdims=True)).astype(BF16) for e in es]
    for p, cs in zip(ps, heads):
        att[:, cs] = _dot(p, vb[:, cs]).astype(BF16)
    o_ref[...] = _layer_norm(ALPHA * x + _dot(att[...], wo_ref[...]), g_ref[...], b_ref[...])


def _cross_attn_ln(x, mk, mv, w_q, w_o, g, b, t, n_mem, tm):
    m, d = x.shape
    tiles_per_seq = t // tm
    mem = pl.BlockSpec((n_mem, d), lambda i: (i // tiles_per_seq, 0))
    tile = pl.BlockSpec((tm, d), lambda i: (i, 0))
    return pl.pallas_call(
        functools.partial(_xattn_ln_kernel, tiles_per_seq=tiles_per_seq),
        grid=(m // tm,),
        in_specs=[tile, mem, mem, w_q.resident(), w_o.resident(),
                  _resident(g.shape), _resident(b.shape)],
        out_specs=tile,
        out_shape=jax.ShapeDtypeStruct((m, d), F32),
        scratch_shapes=[pltpu.VMEM((n_mem, d), BF16), pltpu.VMEM((n_mem, d), BF16),
                        pltpu.VMEM((tm, d), BF16)],
        compiler_params=_params("arbitrary"),
        name="cross_attn_ln",
    )(x, mk, mv, w_q.stack, w_o.stack, g, b)


def _cross_attn(q, mk, mv, n, t, n_mem, mem_block0, tq):
    m, d = q.shape
    nq = t // tq
    mem = pl.BlockSpec((n_mem, d), lambda b, i: (mem_block0 + b, 0))
    return pl.pallas_call(
        _xattn_kernel,
        grid=(n, nq),
        in_specs=[pl.BlockSpec((tq, d), lambda b, i: (b * nq + i, 0)), mem, mem],
        out_specs=pl.BlockSpec((tq, d), lambda b, i: (b * nq + i, 0)),
        out_shape=jax.ShapeDtypeStruct((m, d), BF16),
        scratch_shapes=[pltpu.VMEM((n_mem, d), BF16), pltpu.VMEM((n_mem, d), BF16)],
        compiler_params=_params("arbitrary", "arbitrary"),
        name="cross_attn",
    )(q, mk, mv)


def _ffn_kernel(*refs, seg, tiles_per_seq, carried, tail_rows, sub):
    if carried:
        (x_ref, wa_ref, wg_ref, wd_ref, cw_ref, cb_ref, g_ref, b_ref,
         o_ref, tail_ref, xb, acc, carry) = refs
    else:
        (x_ref, wa_ref, wg_ref, wd_ref, cw_ref, cb_ref, g_ref, b_ref, s0_ref, s1_ref,
         o_ref, tail_ref, xb, acc) = refs
    i, j = pl.program_id(0), pl.program_id(1)
    tm, tf = x_ref.shape[0], wa_ref.shape[1]

    @pl.when(j == 0)
    def _():
        xb[...] = x_ref[...].astype(BF16)
        acc[...] = jnp.zeros_like(acc)

    if carried:
        @pl.when(i == 0)
        def _():
            carry[j] = jnp.zeros((SUBLANES, tf), F32)

    x = xb[...]
    cols = [slice(k * sub, (k + 1) * sub) for k in range(tf // sub)]
    ups = [(_dot(x, wa_ref[:, cs]), _dot(x, wg_ref[:, cs])) for cs in cols]
    t = lax.broadcasted_iota(jnp.int32, (tm, sub), 0) % seg
    for cs, (a, gate) in zip(cols, ups):
        if carried:
            prev = jnp.where(i % tiles_per_seq != 0, carry[j, :, cs], 0.0)
            s0 = prev[SUBLANES - 2:SUBLANES - 1]
            s1 = prev[SUBLANES - 1:SUBLANES]
            carry[j, :, cs] = a[tm - SUBLANES:]
        else:
            s0, s1 = s0_ref[:, cs], s1_ref[:, cs]
        p1 = jnp.where(t == 0, s1, pltpu.roll(a, 1, axis=0))
        p2 = jnp.where(t == 0, s0, jnp.where(t == 1, s1, pltpu.roll(a, 2, axis=0)))
        a_c = cw_ref[0:1, cs] * p2 + cw_ref[1:2, cs] * p1 + cw_ref[2:3, cs] * a + cb_ref[:, cs]
        y = 0.5 * a_c * (1.0 + lax.erf(a_c * math.sqrt(0.5))) * gate
        acc[...] += _dot(y.astype(BF16), wd_ref[cs, :])
        tail_ref[:, cs] = a[tm - tail_rows:]

    @pl.when(j == pl.num_programs(1) - 1)
    def _():
        o_ref[...] = _layer_norm(ALPHA * x_ref[...] + acc[...], g_ref[...], b_ref[...])


def _conv_ffn(x, w_up, cw, cb, w_down, g, b, seg, tm, state_rows=None, tf=512, sub=256):
    m, d = x.shape
    f = w_down.shape[0]
    nj = f // tf
    carried = state_rows is None
    tail_rows = SUBLANES if carried else tm
    tiles_per_seq = max(seg // tm, 1)
    vec_d = pl.BlockSpec((1, d), lambda i, j: (0, 0))
    in_specs = [pl.BlockSpec((tm, d), lambda i, j: (i, 0)),
                w_up.spec((d, tf), lambda i, j: (0, j)),
                w_up.spec((d, tf), lambda i, j: (0, nj + j)),
                w_down.spec((tf, d), lambda i, j: (j, 0)),
                pl.BlockSpec((FFN_CONV_K, tf), lambda i, j: (0, j)),
                pl.BlockSpec((1, tf), lambda i, j: (0, j)),
                vec_d, vec_d]
    args = [x, w_up.stack, w_up.stack, w_down.stack, cw, cb, g, b]
    scratch = [pltpu.VMEM((tm, d), BF16), pltpu.VMEM((tm, d), F32)]
    if carried:
        scratch.append(pltpu.VMEM((nj, SUBLANES, tf), F32))
    else:
        in_specs += [pl.BlockSpec((tm, tf), lambda i, j: (i, j))] * 2
        args += list(state_rows)
    return pl.pallas_call(
        functools.partial(_ffn_kernel, seg=seg, tiles_per_seq=tiles_per_seq, carried=carried,
                          tail_rows=tail_rows, sub=sub),
        grid=(m // tm, nj),
        in_specs=in_specs,
        out_specs=[pl.BlockSpec((tm, d), lambda i, j: (i, 0)),
                   pl.BlockSpec((tail_rows, tf), lambda i, j: (i, j))],
        out_shape=[jax.ShapeDtypeStruct((m, d), F32),
                   jax.ShapeDtypeStruct((m // tm * tail_rows, f), F32)],
        scratch_shapes=scratch,
        compiler_params=_params("arbitrary", "arbitrary"),
        name="conv_ffn",
    )(*args)


def _row(v):
    return v.reshape(1, -1)


def _mixing(x, t, w, hist, attend, tm):
    q, k, v, kb, vb, u, ga, gb = _in_proj(x, w["w_in"], tm)
    att = attend(q, kb, vb)
    lead = CONV_PAD - CONV_STATE
    hist = jnp.pad(hist, ((0, 0), (lead, 0), (0, 0)))
    conv_w = jnp.pad(w["conv_w"], ((lead, CONV_PAD + SUBLANES - lead - CONV_K), (0, 0)))
    cbr = _conv_branch(u, hist, conv_w, w["conv_b"], w["conv_ln_g"], w["conv_ln_b"], t,
                       tt=min(t, 256))
    x = _mix_out(x, att, cbr, ga, gb, w["w_a"], w["w_b"], w["w_o"], w["ln1_g"], w["ln1_b"],
                 min(tm, 256))
    return x, k, v, u


def kernel(x_prompt, x_sample, cache_k, cache_v, cache_mem_k, cache_mem_v, state_conv, state_ffn,
           page_table, mem_prompt, w_in, lam, subln_g, w_a, conv_w, conv_b, conv_ln_g, conv_ln_b,
           w_b, w_o, ln1_g, ln1_b, w_xq, w_xk, w_xv, w_xo, ln2_g, ln2_b, w_up, ffn_conv_w,
           ffn_conv_b, w_down, ln3_g, ln3_b):
    n_p, t_p, d = x_prompt.shape
    n_s, t_s, _ = x_sample.shape
    depth = w_in.shape[0]
    aw = d // 2
    nh = aw // HEAD_W
    n_mem = mem_prompt.shape[1]
    f = w_down.shape[1]
    page = cache_k.shape[2]
    m_p, m_s = n_p * t_p, n_s * t_s
    tm_p, tm_s = min(512, m_p), m_s

    cache_k2 = cache_k.reshape(depth, -1, page * nh, HEAD_W)
    cache_v2 = cache_v.reshape(depth, -1, page * nh, HEAD_W)
    mem_k_s = cache_mem_k.reshape(depth * n_s * n_mem, d)
    mem_v_s = cache_mem_v.reshape(depth * n_s * n_mem, d)
    mem2 = mem_prompt.reshape(n_p * n_mem, d)

    mats = dict(w_in=w_in, w_a=w_a, w_b=w_b, w_o=w_o, w_xq=w_xq, w_xk=w_xk, w_xv=w_xv, w_xo=w_xo,
                w_up=w_up, w_down=w_down)
    mats = {name: stack.astype(BF16) for name, stack in mats.items()}

    xp = x_prompt.reshape(m_p, d)
    xs = x_sample.reshape(m_s, d)
    outs = {name: [] for name in ("kp", "vp", "mkp", "mvp", "cp", "fp", "ks", "vs", "cs", "fs")}
    for l in range(depth):
        lam_init = 0.8 - 0.6 * math.exp(-0.3 * l)
        w = dict({name: _LayerWeight(stack, l) for name, stack in mats.items()},
                 conv_w=conv_w[l], conv_b=_row(conv_b[l]),
                 conv_ln_g=_row(conv_ln_g[l]), conv_ln_b=_row(conv_ln_b[l]),
                 ln1_g=_row(ln1_g[l]), ln1_b=_row(ln1_b[l]), ln2_g=_row(ln2_g[l]),
                 ln2_b=_row(ln2_b[l]), ln3_g=_row(ln3_g[l]), ln3_b=_row(ln3_b[l]),
                 ffn_conv_w=ffn_conv_w[l], ffn_conv_b=_row(ffn_conv_b[l]))

        mk = _proj(mem2, w["w_xk"], F32, min(512, mem2.shape[0]))
        mv = _proj(mem2, w["w_xv"], F32, min(512, mem2.shape[0]))
        attend_p = lambda q, kb, vb: _diff_attn_prompt(
            q, kb, vb, lam[l], subln_g[l].reshape(-1, 1), n_p, t_p, lam_init)
        xp, k, v, u = _mixing(xp, t_p, w, jnp.zeros((n_p, CONV_STATE, aw), F32), attend_p, tm_p)
        xp = _cross_attn_ln(xp, mk, mv, w["w_xq"], w["w_xo"], w["ln2_g"], w["ln2_b"], t_p, n_mem,
                            min(256, t_p))
        xp, tail = _conv_ffn(xp, w["w_up"], w["ffn_conv_w"], w["ffn_conv_b"], w["w_down"],
                             w["ln3_g"], w["ln3_b"], seg=t_p, tm=tm_p)
        outs["kp"].append(k.reshape(n_p, t_p, nh, HEAD_W))
        outs["vp"].append(v.reshape(n_p, t_p, nh, HEAD_W))
        outs["mkp"].append(mk.reshape(n_p, n_mem, X_HEADS, d // X_HEADS))
        outs["mvp"].append(mv.reshape(n_p, n_mem, X_HEADS, d // X_HEADS))
        outs["cp"].append(u.reshape(n_p, t_p, aw)[:, -CONV_STATE:])
        outs["fp"].append(tail.reshape(n_p, t_p // tm_p, SUBLANES, f)[:, -1, -(FFN_CONV_K - 1):])

        def attend_s(q, kb, vb):
            att = _diff_attn_sample(q.reshape(n_s, t_s, aw), kb.reshape(n_s, t_s, aw),
                                    vb.reshape(n_s, t_s, aw), cache_k2, cache_v2, page_table,
                                    lam[l], _row(subln_g[l]), l, lam_init)
            return att.reshape(m_s, aw)

        xs, k, v, u = _mixing(xs, t_s, w, state_conv[l], attend_s, tm_s)
        xq = _proj(xs, w["w_xq"], BF16, tm_s)
        xa = _cross_attn(xq, mem_k_s, mem_v_s, n_s, t_s, n_mem, l * n_s, t_s)
        xs = _proj_ln(xs, xa, w["w_xo"], w["ln2_g"], w["ln2_b"], tm_s)
        st = state_ffn[l]
        s0 = jnp.repeat(st[:, 0], t_s, axis=0)
        s1 = jnp.repeat(st[:, 1], t_s, axis=0)
        xs, a_s = _conv_ffn(xs, w["w_up"], w["ffn_conv_w"], w["ffn_conv_b"], w["w_down"],
                            w["ln3_g"], w["ln3_b"], seg=t_s, tm=tm_s, state_rows=(s0, s1))
        outs["ks"].append(k.reshape(n_s, t_s, nh, HEAD_W))
        outs["vs"].append(v.reshape(n_s, t_s, nh, HEAD_W))
        u_ext = jnp.concatenate([state_conv[l], u.reshape(n_s, t_s, aw)], axis=1)
        outs["cs"].append(u_ext[:, -CONV_STATE:])
        a_ext = jnp.concatenate([st, a_s.reshape(n_s, t_s, f)], axis=1)
        outs["fs"].append(a_ext[:, -(FFN_CONV_K - 1):])

    stk = {name: jnp.stack(vals) for name, vals in outs.items()}
    return (xp.reshape(n_p, t_p, d), xs.reshape(n_s, t_s, d),
            stk["kp"], stk["vp"], stk["mkp"], stk["mvp"], stk["cp"], stk["fp"],
            stk["ks"], stk["vs"], stk["cs"], stk["fs"])
```
